```python
import math
import jax, jax.numpy as jnp
from jax import lax
import numpy as np

D_MODEL = 2048
BATCH = 4
SEQ = 2048
DEPTH = 1
DEC_BATCH = 128
DEC_SEQ = 8
PAST_LEN = 16384
PAGE_SIZE = 128

CHUNK = 128
D_GMLP = D_MODEL
GMLP_GROUPS = 8
NOPE_DIM = 128
ROPE_DIM = 64
V_DIM = 128
N_HEADS = D_MODEL // V_DIM
KV_RANK = 512
Q_DIM = N_HEADS * (NOPE_DIM + ROPE_DIM)
ROPE_THETA = 10000.0
ATTN_SCALE = (NOPE_DIM + ROPE_DIM) ** -0.5
Q_BLOCK = 128
MAX_PAGES_PER_STEP = 8
D_FF = 4 * D_MODEL
LN_EPS = 1e-5
RMS_EPS = 1e-6
DN_ALPHA = (2 * DEPTH) ** 0.25
DN_BETA = (8 * DEPTH) ** -0.25
N_IN = 2 * D_GMLP + Q_DIM + KV_RANK + ROPE_DIM + 2 * D_MODEL

kernel_name = 'mla_gmlp_gated_hybrid_step'


def _layernorm(x, g, b):
    xf = x.astype(jnp.float32)
    mu = jnp.mean(xf, -1, keepdims=True)
    var = jnp.mean(jnp.square(xf - mu), -1, keepdims=True)
    return ((xf - mu) * lax.rsqrt(var + LN_EPS)).astype(x.dtype) * g + b


def _rmsnorm(x, g):
    xf = x.astype(jnp.float32)
    return (xf * lax.rsqrt(jnp.mean(xf * xf, -1, keepdims=True) + RMS_EPS)).astype(x.dtype) * g


def _rope_tables(pos, dtype):
    inv = ROPE_THETA ** (-jnp.arange(0, ROPE_DIM, 2, dtype=jnp.float32) / ROPE_DIM)
    ang = pos.astype(jnp.float32)[:, None] * inv[None, :]
    return jnp.cos(ang).astype(dtype), jnp.sin(ang).astype(dtype)


def _apply_rope(x, cos, sin):
    x1, x2 = jnp.split(x, 2, axis=-1)
    return jnp.concatenate([x1 * cos - x2 * sin, x1 * sin + x2 * cos], axis=-1)


def _project(x, pos, w_in, gmlp_ln_g, gmlp_ln_b, kv_norm_g, w_uk):
    B, S, _ = x.shape
    z = jnp.einsum('bsd,de->bse', x, w_in)
    sizes = (D_GMLP, D_GMLP, Q_DIM, KV_RANK, ROPE_DIM, D_MODEL, D_MODEL)
    offs, acc = [], 0
    for s in sizes[:-1]:
        acc += s
        offs.append(acc)
    u, v, q, c_kv, k_r, g_a, g_b = jnp.split(z, offs, axis=-1)
    u = jax.nn.gelu(u, approximate=False)
    v = _layernorm(jax.nn.gelu(v, approximate=False), gmlp_ln_g, gmlp_ln_b)
    q = q.reshape(B, S, N_HEADS, NOPE_DIM + ROPE_DIM)
    q_nope, q_rope = q[..., :NOPE_DIM], q[..., NOPE_DIM:]
    cos, sin = _rope_tables(pos, x.dtype)
    q_rope = _apply_rope(q_rope, cos[:, None, :], sin[:, None, :])
    k_r = _apply_rope(k_r, cos, sin)
    c_kv = _rmsnorm(c_kv, kv_norm_g)
    q_lat = jnp.einsum('bshn,rhn->bshr', q_nope, w_uk)
    return u, v, q_lat, q_rope, c_kv, k_r, g_a, g_b


def _spatial_gate(u, v, ws, bs):
    B, S, _ = v.shape
    L = min(S, CHUNK)
    mask = jnp.tril(jnp.ones((L, L), dtype=bool))
    w = jnp.where(mask[None], ws[:, :L, :L], jnp.zeros((), ws.dtype))
    vc = v.reshape(B, S // L, L, GMLP_GROUPS, D_GMLP // GMLP_GROUPS)
    mixed = jnp.einsum('gij,bcjgd->bcigd', w, vc) + jnp.transpose(bs[:, :L])[None, None, :, :, None]
    return u * mixed.reshape(B, S, D_GMLP)


def _mla_scores(q_lat, q_rope, c, kr):
    s = jnp.einsum('bqhr,bkr->bhqk', q_lat, c) + jnp.einsum('bqhe,bke->bhqk', q_rope, kr)
    return s.astype(jnp.float32) * ATTN_SCALE


def _mla_prompt(q_lat, q_rope, c_kv, k_r):
    B, S = q_lat.shape[:2]
    nb = S // Q_BLOCK
    qlb = q_lat.reshape(B, nb, Q_BLOCK, N_HEADS, KV_RANK).transpose(1, 0, 2, 3, 4)
    qrb = q_rope.reshape(B, nb, Q_BLOCK, N_HEADS, ROPE_DIM).transpose(1, 0, 2, 3, 4)
    kpos = jnp.arange(S)

    def block(args):
        ql, qr, i = args
        s = _mla_scores(ql, qr, c_kv, k_r)
        qpos = i * Q_BLOCK + jnp.arange(Q_BLOCK)
        s = jnp.where(kpos[None, :] <= qpos[:, None], s, -jnp.inf)
        p = jax.nn.softmax(s, axis=-1).astype(c_kv.dtype)
        return jnp.einsum('bhqk,bkr->bqhr', p, c_kv)

    o = lax.map(block, (qlb, qrb, jnp.arange(nb)))
    return o.transpose(1, 0, 2, 3, 4).reshape(B, S, N_HEADS, KV_RANK)


def _mla_sample(q_lat, q_rope, c_new, kr_new, cache_c, cache_kr, page_table, layer):
    Bd, T = q_lat.shape[:2]
    n_pages = page_table.shape[1]
    bp = math.gcd(n_pages, MAX_PAGES_PER_STEP)
    pt = page_table.reshape(Bd, n_pages // bp, bp).transpose(1, 0, 2)
    s_new = _mla_scores(q_lat, q_rope, c_new, kr_new)
    s_new = jnp.where(jnp.tril(jnp.ones((T, T), dtype=bool)), s_new, -jnp.inf)
    m0 = jnp.max(s_new, axis=-1)
    p0 = jnp.exp(s_new - m0[..., None])
    l0 = jnp.sum(p0, axis=-1)
    acc0 = jnp.einsum('bhqk,bkr->bhqr', p0, c_new.astype(jnp.float32))

    def step(carry, pages):
        m, l, acc = carry
        c = cache_c[layer, pages].reshape(Bd, bp * PAGE_SIZE, KV_RANK)
        kr = cache_kr[layer, pages].reshape(Bd, bp * PAGE_SIZE, ROPE_DIM)
        s = _mla_scores(q_lat, q_rope, c, kr)
        m_new = jnp.maximum(m, jnp.max(s, axis=-1))
        corr = jnp.exp(m - m_new)
        p = jnp.exp(s - m_new[..., None])
        l = l * corr + jnp.sum(p, axis=-1)
        acc = acc * corr[..., None] + jnp.einsum('bhqk,bkr->bhqr', p, c.astype(jnp.float32))
        return (m_new, l, acc), None

    (m, l, acc), _ = lax.scan(step, (m0, l0, acc0), pt)
    o = (acc / l[..., None]).astype(q_lat.dtype)
    return o.transpose(0, 2, 1, 3)


def _merge_and_ffn(x, a, o_lat, g_a, g_b, w_uv, w_o, ln1_g, ln1_b, w_up, w_down, ln2_g, ln2_b):
    B, S, _ = x.shape
    o = jnp.einsum('bshr,rhv->bshv', o_lat, w_uv).reshape(B, S, N_HEADS * V_DIM)
    h = jax.nn.sigmoid(g_a) * a + jax.nn.sigmoid(g_b) * o
    x1 = _layernorm(DN_ALPHA * x + h @ w_o, ln1_g, ln1_b)
    f = jnp.square(jax.nn.relu(x1 @ w_up)) @ w_down
    return _layernorm(DN_ALPHA * x1 + f, ln2_g, ln2_b)


def setup_inputs(seed: int = 0) -> dict:
    key = jax.random.key(seed)
    ks = jax.random.split(key, 24)
    n_pages = PAST_LEN // PAGE_SIZE
    n_used = DEC_BATCH * n_pages
    n_phys = (n_used * 5) // 4

    def nrm(k, shape, scale):
        return jax.random.normal(k, shape, jnp.float32) * scale

    perm = jax.random.permutation(ks[4], n_phys)
    page_table = perm[:n_used].reshape(DEC_BATCH, n_pages).astype(jnp.int32)
    return {
        'x_prompt': nrm(ks[0], (BATCH, SEQ, D_MODEL), 1.0),
        'x_sample': nrm(ks[1], (DEC_BATCH, DEC_SEQ, D_MODEL), 1.0),
        'cache_kv_latent': nrm(ks[2], (DEPTH, n_phys, PAGE_SIZE, KV_RANK), 1.0),
        'cache_k_rope': nrm(ks[3], (DEPTH, n_phys, PAGE_SIZE, ROPE_DIM), 1.0),
        'page_table': page_table,
        'w_in': nrm(ks[5], (DEPTH, D_MODEL, N_IN), D_MODEL ** -0.5),
        'gmlp_ln_g': 1.0 + nrm(ks[6], (DEPTH, D_GMLP), 0.02),
        'gmlp_ln_b': nrm(ks[7], (DEPTH, D_GMLP), 0.02),
        'gmlp_ws': nrm(ks[8], (DEPTH, GMLP_GROUPS, CHUNK, CHUNK), CHUNK ** -0.5),
        'gmlp_bs': 1.0 + nrm(ks[9], (DEPTH, GMLP_GROUPS, CHUNK), 0.02),
        'kv_norm_g': 1.0 + nrm(ks[10], (DEPTH, KV_RANK), 0.02),
        'w_uk': nrm(ks[11], (DEPTH, KV_RANK, N_HEADS, NOPE_DIM), KV_RANK ** -0.5),
        'w_uv': nrm(ks[12], (DEPTH, KV_RANK, N_HEADS, V_DIM), DN_BETA * KV_RANK ** -0.5),
        'w_o': nrm(ks[13], (DEPTH, D_MODEL, D_MODEL), DN_BETA * D_MODEL ** -0.5),
        'ln1_g': 1.0 + nrm(ks[14], (DEPTH, D_MODEL), 0.02),
        'ln1_b': nrm(ks[15], (DEPTH, D_MODEL), 0.02),
        'w_up': nrm(ks[16], (DEPTH, D_MODEL, D_FF), D_MODEL ** -0.5),
        'w_down': nrm(ks[17], (DEPTH, D_FF, D_MODEL), DN_BETA * D_FF ** -0.5),
        'ln2_g': 1.0 + nrm(ks[18], (DEPTH, D_MODEL), 0.02),
        'ln2_b': nrm(ks[19], (DEPTH, D_MODEL), 0.02),
    }


def reference(x_prompt, x_sample, cache_kv_latent, cache_k_rope, page_table,
              w_in, gmlp_ln_g, gmlp_ln_b, gmlp_ws, gmlp_bs, kv_norm_g, w_uk, w_uv,
              w_o, ln1_g, ln1_b, w_up, w_down, ln2_g, ln2_b):
    S = x_prompt.shape[1]
    T = x_sample.shape[1]
    past = page_table.shape[1] * PAGE_SIZE
    pos_p = jnp.arange(S)
    pos_s = past + jnp.arange(T)
    hp, hs = x_prompt, x_sample
    c_p_l, kr_p_l, c_s_l, kr_s_l, v_s_l = [], [], [], [], []
    for l in range(DEPTH):
        u, v, q_lat, q_rope, c_kv, k_r, g_a, g_b = _project(
            hp, pos_p, w_in[l], gmlp_ln_g[l], gmlp_ln_b[l], kv_norm_g[l], w_uk[l])
        a = _spatial_gate(u, v, gmlp_ws[l], gmlp_bs[l])
        o_lat = _mla_prompt(q_lat, q_rope, c_kv, k_r)
        hp = _merge_and_ffn(hp, a, o_lat, g_a, g_b, w_uv[l], w_o[l], ln1_g[l], ln1_b[l],
                            w_up[l], w_down[l], ln2_g[l], ln2_b[l])
        c_p_l.append(c_kv)
        kr_p_l.append(k_r)
        u, v, q_lat, q_rope, c_kv, k_r, g_a, g_b = _project(
            hs, pos_s, w_in[l], gmlp_ln_g[l], gmlp_ln_b[l], kv_norm_g[l], w_uk[l])
        a = _spatial_gate(u, v, gmlp_ws[l], gmlp_bs[l])
        o_lat = _mla_sample(q_lat, q_rope, c_kv, k_r, cache_kv_latent, cache_k_rope,
                            page_table, l)
        hs = _merge_and_ffn(hs, a, o_lat, g_a, g_b, w_uv[l], w_o[l], ln1_g[l], ln1_b[l],
                            w_up[l], w_down[l], ln2_g[l], ln2_b[l])
        c_s_l.append(c_kv)
        kr_s_l.append(k_r)
        v_s_l.append(v)
    new_kv_lat_prompt = jnp.stack(c_p_l)
    new_k_rope_prompt = jnp.stack(kr_p_l)
    new_kv_lat_sample = jnp.stack(c_s_l)
    new_k_rope_sample = jnp.stack(kr_s_l)
    new_gmlp_v_sample = jnp.stack(v_s_l)
    return (hp, hs, new_kv_lat_prompt, new_k_rope_prompt, new_kv_lat_sample, new_k_rope_sample, new_gmlp_v_sample)
```

```python
import functools
import math

import jax
import jax.numpy as jnp
from jax import lax
from jax.experimental import pallas as pl
from jax.experimental.pallas import tpu as pltpu

F32 = jnp.float32
BF16 = jnp.bfloat16

LANES = 128
ROPE_THETA = 10000.0
LN_EPS = 1e-5
RMS_EPS = 1e-6
MAX_PAGES_PER_STEP = 16
VMEM_LIMIT = 56 * 1024 * 1024


def _cparams(*sem):
    return pltpu.CompilerParams(dimension_semantics=sem, vmem_limit_bytes=VMEM_LIMIT)


def _const_spec(shape):
    nd = len(shape)
    return pl.BlockSpec(shape, lambda *_: (0,) * nd, pipeline_mode=pl.Buffered(1))


def _gelu(x):
    return 0.5 * x * (1.0 + lax.erf(x * math.sqrt(0.5)))


def _layernorm(x, g, b):
    mu = jnp.mean(x, -1, keepdims=True)
    d = x - mu
    var = jnp.mean(d * d, -1, keepdims=True)
    return d * lax.rsqrt(var + LN_EPS) * g + b


def _gate_kernel(x_ref, wu_ref, wv_ref, lng_ref, lnb_ref, ws_ref, bs_ref, a_ref, *v_out,
                 seq_len, chunk, groups):
    x = x_ref[...].astype(BF16)
    u = _gelu(jnp.dot(x, wu_ref[...], preferred_element_type=F32))
    v = _layernorm(_gelu(jnp.dot(x, wv_ref[...], preferred_element_type=F32)),
                   lng_ref[...], lnb_ref[...])
    if v_out:
        v_out[0][...] = v
    vb = v.astype(BF16)
    tm, d = v.shape
    gw = d // groups
    row = lax.broadcasted_iota(jnp.int32, (chunk, chunk), 0)
    col = lax.broadcasted_iota(jnp.int32, (chunk, chunk), 1)
    mask = (col <= row) & ((row // seq_len) == (col // seq_len))
    for g in range(groups):
        w = jnp.where(mask, ws_ref[g], 0.0).astype(BF16)
        for c in range(tm // chunk):
            rs, cs = slice(c * chunk, (c + 1) * chunk), slice(g * gw, (g + 1) * gw)
            mixed = jnp.dot(w, vb[rs, cs], preferred_element_type=F32) + bs_ref[g]
            a_ref[rs, cs] = (u[rs, cs] * mixed).astype(a_ref.dtype)


def _gate_call(x, wu, wv, lng, lnb, ws, bs, *, seq_len, want_v, tm):
    t, d = x.shape
    groups, chunk, _ = ws.shape
    gw = d // groups
    out_shape = [jax.ShapeDtypeStruct((t, d), BF16)]
    out_specs = [pl.BlockSpec((tm, d), lambda i: (i, 0))]
    if want_v:
        out_shape.append(jax.ShapeDtypeStruct((t, d), F32))
        out_specs.append(pl.BlockSpec((tm, d), lambda i: (i, 0)))
    return pl.pallas_call(
        functools.partial(_gate_kernel, seq_len=seq_len, chunk=chunk, groups=groups),
        grid=(t // tm,),
        in_specs=[
            pl.BlockSpec((tm, d), lambda i: (i, 0)),
            _const_spec((d, d)), _const_spec((d, d)),
            _const_spec((1, d)), _const_spec((1, d)),
            _const_spec((groups, chunk, chunk)), _const_spec((groups, chunk, gw)),
        ],
        out_specs=out_specs,
        out_shape=out_shape,
        compiler_params=_cparams("parallel"),
        name="gate",
    )(x, wu, wv, lng, lnb, ws, bs)


def _qkv_kernel(x_ref, w_ref, kvg_ref, cos_ref, sin_ref, wuk_ref,
                q_ref, ckv_ref, kr_ref, kvb_ref, *, n_heads, nope, rank, rope, scale):
    x = x_ref[...].astype(BF16)
    z = jnp.dot(x, w_ref[...], preferred_element_type=F32)
    tm = z.shape[0]
    nope_w = n_heads * nope
    c = z[:, nope_w:nope_w + rank]
    c = c * lax.rsqrt(jnp.mean(c * c, -1, keepdims=True) + RMS_EPS) * kvg_ref[...]
    ckv_ref[...] = c
    kvb_ref[:, :rank] = c.astype(BF16)

    off = nope_w + rank
    cos, sin = cos_ref[...], sin_ref[...]
    lane = lax.broadcasted_iota(jnp.int32, (tm, LANES), 1)
    first_half = (lane % rope) < (rope // 2)

    def rotate(xg):
        swapped = jnp.where(first_half, pltpu.roll(xg, LANES - rope // 2, 1),
                            pltpu.roll(xg, rope // 2, 1))
        return xg * cos + swapped * sin

    for k in range(n_heads // 2):
        pair = rotate(z[:, off + LANES * k: off + LANES * (k + 1)]) * scale
        for h, slot in ((2 * k, pair), (2 * k + 1, pltpu.roll(pair, rope, 1))):
            ql = jnp.dot(z[:, h * nope:(h + 1) * nope].astype(BF16), wuk_ref[h],
                         preferred_element_type=F32) * scale
            q_ref[h, :, :rank] = ql.astype(q_ref.dtype)
            q_ref[h, :, rank:] = slot.astype(q_ref.dtype)
    kr_off = off + LANES * (n_heads // 2)
    kr = rotate(z[:, kr_off:kr_off + LANES])
    kr_ref[...] = kr[:, :rope]
    kvb_ref[:, rank:] = kr.astype(BF16)


def _qkv_call(x, w, kvg, cos, sin, wuk, *, q_dtype, table_blocks, scale, tm):
    t, d = x.shape
    n_heads, nope, rank = wuk.shape
    rope = LANES // 2
    qw = rank + LANES
    ncols = w.shape[1]
    return pl.pallas_call(
        functools.partial(_qkv_kernel, n_heads=n_heads, nope=nope, rank=rank, rope=rope,
                          scale=scale),
        grid=(t // tm,),
        in_specs=[
            pl.BlockSpec((tm, d), lambda i: (i, 0)),
            _const_spec((d, ncols)), _const_spec((1, rank)),
            pl.BlockSpec((tm, LANES), lambda i: (i % table_blocks, 0)),
            pl.BlockSpec((tm, LANES), lambda i: (i % table_blocks, 0)),
            _const_spec((n_heads, nope, rank)),
        ],
        out_specs=[
            pl.BlockSpec((n_heads, tm, qw), lambda i: (0, i, 0)),
            pl.BlockSpec((tm, rank), lambda i: (i, 0)),
            pl.BlockSpec((tm, rope), lambda i: (i, 0)),
            pl.BlockSpec((tm, qw), lambda i: (i, 0)),
        ],
        out_shape=[
            jax.ShapeDtypeStruct((n_heads, t, qw), q_dtype),
            jax.ShapeDtypeStruct((t, rank), F32),
            jax.ShapeDtypeStruct((t, rope), F32),
            jax.ShapeDtypeStruct((t, qw), BF16),
        ],
        compiler_params=_cparams("parallel"),
        name="qkv",
    )(x, w, kvg, cos, sin, wuk)


def _attn_prompt_kernel(q_ref, kv_ref, wuv_ref, o_ref, m_ref, l_ref, acc_ref, *, tk, rank):
    qi = pl.program_id(1)
    n_heads, tq, qw = q_ref.shape
    rows = n_heads * tq
    q = q_ref[...].reshape(rows, qw)
    m_ref[...] = jnp.full(m_ref.shape, -jnp.inf, F32)
    l_ref[...] = jnp.zeros(l_ref.shape, F32)
    acc_ref[...] = jnp.zeros(acc_ref.shape, F32)
    qpos = qi * tq + lax.broadcasted_iota(jnp.int32, (rows, tk), 0) % tq
    kcol = lax.broadcasted_iota(jnp.int32, (rows, tk), 1)

    def body(c, carry):
        start = pl.multiple_of(c * tk, tk)
        k = kv_ref[pl.ds(start, tk), :]
        s = lax.dot_general(q, k, (((1,), (1,)), ((), ())), preferred_element_type=F32)
        s = jnp.where(kcol + start <= qpos, s, -jnp.inf)
        m_prev = m_ref[...]
        m_new = jnp.maximum(m_prev, jnp.max(s, -1, keepdims=True))
        p = jnp.exp(s - m_new)
        corr = jnp.exp(m_prev - m_new)
        l_ref[...] = l_ref[...] * corr + jnp.sum(p, -1, keepdims=True)
        acc_ref[...] = acc_ref[...] * corr + jnp.dot(p.astype(BF16), k[:, :rank],
                                                     preferred_element_type=F32)
        m_ref[...] = m_new
        return carry

    lax.fori_loop(0, (qi * tq + tq + tk - 1) // tk, body, 0)
    o_lat = (acc_ref[...] / l_ref[...]).astype(BF16)
    vd = wuv_ref.shape[2]
    for h in range(n_heads):
        o_ref[:, h * vd:(h + 1) * vd] = jnp.dot(
            o_lat[h * tq:(h + 1) * tq], wuv_ref[h], preferred_element_type=F32
        ).astype(o_ref.dtype)


def _attn_prompt_call(q, kvb, wuv, *, batch, seq, tq, tk):
    n_heads, t, qw = q.shape
    _, rank, vd = wuv.shape
    nq = seq // tq
    rows = n_heads * tq
    return pl.pallas_call(
        functools.partial(_attn_prompt_kernel, tk=tk, rank=rank),
        grid=(batch, nq),
        in_specs=[
            pl.BlockSpec((n_heads, tq, qw), lambda b, i: (0, b * nq + i, 0)),
            pl.BlockSpec((seq, qw), lambda b, i: (b, 0)),
            _const_spec((n_heads, rank, vd)),
        ],
        out_specs=pl.BlockSpec((tq, n_heads * vd), lambda b, i: (b * nq + i, 0)),
        out_shape=jax.ShapeDtypeStruct((t, n_heads * vd), BF16),
        scratch_shapes=[
            pltpu.VMEM((rows, 1), F32), pltpu.VMEM((rows, 1), F32),
            pltpu.VMEM((rows, rank), F32),
        ],
        compiler_params=_cparams("parallel", "parallel"),
        name="attn_prompt",
    )(q, kvb, wuv)


def _attn_sample_kernel(pt_ref, q_ref, cnew_ref, krnew_ref, wuv_ref, *rest, pages, rank, rope):
    c_refs, kr_refs = rest[:pages], rest[pages:2 * pages]
    o_ref, m_ref, l_ref, acc_ref, kc_ref, krc_ref = rest[2 * pages:]
    j = pl.program_id(1)
    n_heads, tn, qw = q_ref.shape
    rows = n_heads * tn
    page = c_refs[0].shape[0]
    q = q_ref[...].reshape(rows, qw)
    q_lat = q[:, :rank].astype(BF16)
    q_rope = q[:, rank:rank + rope].astype(BF16)

    def scores(kc, krc):
        dn = (((1,), (1,)), ((), ()))
        return (lax.dot_general(q_lat, kc, dn, preferred_element_type=F32)
                + lax.dot_general(q_rope, krc, dn, preferred_element_type=F32))

    def update(s, kc):
        m_prev = m_ref[...]
        m_new = jnp.maximum(m_prev, jnp.max(s, -1, keepdims=True))
        p = jnp.exp(s - m_new)
        corr = jnp.exp(m_prev - m_new)
        l_ref[...] = l_ref[...] * corr + jnp.sum(p, -1, keepdims=True)
        acc_ref[...] = acc_ref[...] * corr + jnp.dot(p.astype(BF16), kc,
                                                     preferred_element_type=F32)
        m_ref[...] = m_new

    @pl.when(j == 0)
    def _():
        m_ref[...] = jnp.full(m_ref.shape, -jnp.inf, F32)
        l_ref[...] = jnp.zeros(l_ref.shape, F32)
        acc_ref[...] = jnp.zeros(acc_ref.shape, F32)
        pad = page - tn
        kc = jnp.concatenate([cnew_ref[...], jnp.zeros((pad, rank), F32)], 0).astype(BF16)
        krc = jnp.concatenate([krnew_ref[...], jnp.zeros((pad, rope), F32)], 0).astype(BF16)
        s = scores(kc, krc)
        tok = lax.broadcasted_iota(jnp.int32, (rows, page), 0) % tn
        key = lax.broadcasted_iota(jnp.int32, (rows, page), 1)
        update(jnp.where(key <= tok, s, -jnp.inf), kc)

    for g in range(pages):
        kc_ref[g * page:(g + 1) * page, :] = c_refs[g][...].astype(BF16)
        krc_ref[g * page:(g + 1) * page, :] = kr_refs[g][...].astype(BF16)
    kc = kc_ref[...]
    update(scores(kc, krc_ref[...]), kc)

    @pl.when(j == pl.num_programs(1) - 1)
    def _():
        o_lat = (acc_ref[...] / l_ref[...]).astype(BF16)
        vd = wuv_ref.shape[2]
        for h in range(n_heads):
            o_ref[:, h * vd:(h + 1) * vd] = jnp.dot(
                o_lat[h * tn:(h + 1) * tn], wuv_ref[h], preferred_element_type=F32
            ).astype(o_ref.dtype)


def _attn_sample_call(page_table, q, c_new, kr_new, wuv, cache_c, cache_kr, *, layer, tn):
    n_heads, t, qw = q.shape
    _, rank, vd = wuv.shape
    rope = kr_new.shape[1]
    bd, n_pages = page_table.shape
    page = cache_c.shape[2]
    pages = math.gcd(n_pages, MAX_PAGES_PER_STEP)
    rows = n_heads * tn

    def cache_spec(width, g):
        return pl.BlockSpec((None, None, page, width),
                            lambda b, j, pt: (layer, pt[b, j * pages + g], 0, 0))

    grid_spec = pltpu.PrefetchScalarGridSpec(
        num_scalar_prefetch=1,
        grid=(bd, n_pages // pages),
        in_specs=[
            pl.BlockSpec((n_heads, tn, qw), lambda b, j, pt: (0, b, 0)),
            pl.BlockSpec((tn, rank), lambda b, j, pt: (b, 0)),
            pl.BlockSpec((tn, rope), lambda b, j, pt: (b, 0)),
            pl.BlockSpec((n_heads, rank, vd), lambda b, j, pt: (0, 0, 0)),
        ] + [cache_spec(rank, g) for g in range(pages)]
          + [cache_spec(rope, g) for g in range(pages)],
        out_specs=pl.BlockSpec((tn, n_heads * vd), lambda b, j, pt: (b, 0)),
        scratch_shapes=[
            pltpu.VMEM((rows, 1), F32), pltpu.VMEM((rows, 1), F32),
            pltpu.VMEM((rows, rank), F32),
            pltpu.VMEM((pages * page, rank), BF16), pltpu.VMEM((pages * page, rope), BF16),
        ],
    )
    return pl.pallas_call(
        functools.partial(_attn_sample_kernel, pages=pages, rank=rank, rope=rope),
        grid_spec=grid_spec,
        out_shape=jax.ShapeDtypeStruct((t, n_heads * vd), BF16),
        compiler_params=_cparams("parallel", "arbitrary"),
        name="attn_sample",
    )(page_table, q, c_new, kr_new, wuv, *([cache_c] * pages), *([cache_kr] * pages))


def _merge_kernel(x_ref, a_ref, o_ref, wg_ref, wo_ref, g_ref, b_ref, x1_ref, *, alpha):
    x = x_ref[...]
    d = x.shape[1]
    gates = jax.nn.sigmoid(jnp.dot(x.astype(BF16), wg_ref[...], preferred_element_type=F32))
    h = gates[:, :d] * a_ref[...].astype(F32) + gates[:, d:] * o_ref[...].astype(F32)
    t = jnp.dot(h.astype(BF16), wo_ref[...], preferred_element_type=F32)
    x1_ref[...] = _layernorm(alpha * x + t, g_ref[...], b_ref[...])


def _merge_call(x, a, o, wg, wo, g, b, *, alpha, tm):
    t, d = x.shape
    tok = pl.BlockSpec((tm, d), lambda i: (i, 0))
    return pl.pallas_call(
        functools.partial(_merge_kernel, alpha=alpha),
        grid=(t // tm,),
        in_specs=[tok, tok, tok, _const_spec((d, 2 * d)), _const_spec((d, d)),
                  _const_spec((1, d)), _const_spec((1, d))],
        out_specs=tok,
        out_shape=jax.ShapeDtypeStruct((t, d), F32),
        compiler_params=_cparams("parallel"),
        name="merge",
    )(x, a, o, wg, wo, g, b)


def _ffn_kernel(x1_ref, wup_ref, wdn_ref, g_ref, b_ref, y_ref, xb_ref, acc_ref, *, alpha):
    j = pl.program_id(1)

    @pl.when(j == 0)
    def _():
        xb_ref[...] = x1_ref[...].astype(BF16)
        acc_ref[...] = jnp.zeros(acc_ref.shape, F32)

    f = jnp.maximum(jnp.dot(xb_ref[...], wup_ref[...], preferred_element_type=F32), 0.0)
    acc_ref[...] += jnp.dot((f * f).astype(BF16), wdn_ref[...], preferred_element_type=F32)

    @pl.when(j == pl.num_programs(1) - 1)
    def _():
        y_ref[...] = _layernorm(alpha * x1_ref[...] + acc_ref[...], g_ref[...], b_ref[...])


def _ffn_call(x1, wup, wdn, g, b, *, alpha, tm, tf):
    t, d = x1.shape
    dff = wup.shape[1]
    return pl.pallas_call(
        functools.partial(_ffn_kernel, alpha=alpha),
        grid=(t // tm, dff // tf),
        in_specs=[
            pl.BlockSpec((tm, d), lambda i, j: (i, 0)),
            pl.BlockSpec((d, tf), lambda i, j: (0, j)),
            pl.BlockSpec((tf, d), lambda i, j: (j, 0)),
            _const_spec((1, d)), _const_spec((1, d)),
        ],
        out_specs=pl.BlockSpec((tm, d), lambda i, j: (i, 0)),
        out_shape=jax.ShapeDtypeStruct((t, d), F32),
        scratch_shapes=[pltpu.VMEM((tm, d), BF16), pltpu.VMEM((tm, d), F32)],
        compiler_params=_cparams("parallel", "arbitrary"),
        name="ffn",
    )(x1, wup, wdn, g, b)


def _rope_tables(pos, rope):
    inv = ROPE_THETA ** (-jnp.arange(0, rope, 2, dtype=F32) / rope)
    ang = pos.astype(F32)[:, None] * inv[None, :]
    cos, sin = jnp.cos(ang), jnp.sin(ang)
    reps = LANES // rope
    return (jnp.tile(jnp.concatenate([cos, cos], -1), (1, reps)),
            jnp.tile(jnp.concatenate([-sin, sin], -1), (1, reps)))


def _tile_rows(t, want):
    return want if t % want == 0 else t


def kernel(x_prompt, x_sample, cache_kv_latent, cache_k_rope, page_table, w_in, gmlp_ln_g, gmlp_ln_b, gmlp_ws, gmlp_bs, kv_norm_g, w_uk, w_uv, w_o, ln1_g, ln1_b, w_up, w_down, ln2_g, ln2_b):
    batch, seq, d = x_prompt.shape
    bd, tn, _ = x_sample.shape
    depth = w_in.shape[0]
    _, rank, n_heads, nope = w_uk.shape
    rope = cache_k_rope.shape[-1]
    dg = gmlp_ln_g.shape[-1]
    groups, chunk = gmlp_ws.shape[1], gmlp_ws.shape[2]
    page = cache_kv_latent.shape[2]
    assert nope == LANES and 2 * rope == LANES and rank % LANES == 0 and dg == d
    assert n_heads % 2 == 0 and chunk % tn == 0 and page >= tn
    alpha = (2 * depth) ** 0.25
    scale = (nope + rope) ** -0.5
    q_dim = n_heads * (nope + rope)
    o_q, o_c, o_kr, o_g = 2 * dg, 2 * dg + q_dim, 2 * dg + q_dim + rank, 2 * dg + q_dim + rank + rope

    head_base = o_q + (nope + rope) * jnp.arange(n_heads)
    cols = jnp.concatenate([
        (head_base[:, None] + jnp.arange(nope)[None, :]).reshape(-1),
        o_c + jnp.arange(rank),
        (head_base[:, None] + nope + jnp.arange(rope)[None, :]).reshape(-1),
        o_kr + jnp.arange(rope),
    ])

    past = page_table.shape[1] * page
    cos_p, sin_p = _rope_tables(jnp.arange(seq), rope)
    tm_s = _tile_rows(bd * tn, 256)
    cos_s, sin_s = _rope_tables(past + jnp.arange(tn), rope)
    cos_s, sin_s = jnp.tile(cos_s, (tm_s // tn, 1)), jnp.tile(sin_s, (tm_s // tn, 1))

    hp = x_prompt.reshape(batch * seq, d)
    hs = x_sample.reshape(bd * tn, d)
    tm_p = _tile_rows(batch * seq, 256)
    outs = {k: [] for k in ("c_p", "kr_p", "c_s", "kr_s", "v_s")}
    for l in range(depth):
        wl = w_in[l]
        wu, wv = wl[:, :dg].astype(BF16), wl[:, dg:2 * dg].astype(BF16)
        wq = jnp.pad(wl[:, cols], ((0, 0), (0, LANES - rope))).astype(BF16)
        wg = wl[:, o_g:].astype(BF16)
        wuk = jnp.transpose(w_uk[l], (1, 2, 0)).astype(BF16)
        wuv = jnp.transpose(w_uv[l], (1, 0, 2)).astype(BF16)
        wo, wup, wdn = w_o[l].astype(BF16), w_up[l].astype(BF16), w_down[l].astype(BF16)
        lng, lnb = gmlp_ln_g[l][None], gmlp_ln_b[l][None]
        kvg = kv_norm_g[l][None]
        gw = dg // groups
        ws_p = gmlp_ws[l]
        bs_p = jnp.broadcast_to(gmlp_bs[l][:, :, None], (groups, chunk, gw))
        rep = chunk // tn
        ws_s = jnp.tile(gmlp_ws[l][:, :tn, :tn], (1, rep, rep))
        bs_s = jnp.broadcast_to(jnp.tile(gmlp_bs[l][:, :tn], (1, rep))[:, :, None], (groups, chunk, gw))

        a = _gate_call(hp, wu, wv, lng, lnb, ws_p, bs_p, seq_len=min(seq, chunk), want_v=False, tm=tm_p)[0]
        q, c_kv, k_r, kvb = _qkv_call(hp, wq, kvg, cos_p, sin_p, wuk, q_dtype=BF16,
                                      table_blocks=seq // tm_p, scale=scale, tm=tm_p)
        o = _attn_prompt_call(q, kvb, wuv, batch=batch, seq=seq, tq=LANES, tk=256)
        x1 = _merge_call(hp, a, o, wg, wo, ln1_g[l][None], ln1_b[l][None], alpha=alpha, tm=tm_p)
        hp = _ffn_call(x1, wup, wdn, ln2_g[l][None], ln2_b[l][None], alpha=alpha,
                       tm=_tile_rows(batch * seq, 512), tf=512)
        outs["c_p"].append(c_kv.reshape(batch, seq, rank))
        outs["kr_p"].append(k_r.reshape(batch, seq, rope))

        a, v = _gate_call(hs, wu, wv, lng, lnb, ws_s, bs_s, seq_len=tn, want_v=True, tm=tm_s)
        q, c_kv, k_r, _ = _qkv_call(hs, wq, kvg, cos_s, sin_s, wuk, q_dtype=F32,
                                    table_blocks=1, scale=scale, tm=tm_s)
        o = _attn_sample_call(page_table, q, c_kv, k_r, wuv, cache_kv_latent, cache_k_rope,
                              layer=l, tn=tn)
        x1 = _merge_call(hs, a, o, wg, wo, ln1_g[l][None], ln1_b[l][None], alpha=alpha, tm=tm_s)
        hs = _ffn_call(x1, wup, wdn, ln2_g[l][None], ln2_b[l][None], alpha=alpha,
                       tm=_tile_rows(bd * tn, 512), tf=512)
        outs["c_s"].append(c_kv.reshape(bd, tn, rank))
        outs["kr_s"].append(k_r.reshape(bd, tn, rope))
        outs["v_s"].append(v.reshape(bd, tn, dg))

    return (hp.reshape(batch, seq, d), hs.reshape(bd, tn, d),
            jnp.stack(outs["c_p"]), jnp.stack(outs["kr_p"]),
            jnp.stack(outs["c_s"]), jnp.stack(outs["kr_s"]), jnp.stack(outs["v_s"]))
```

```python
import functools
import math

import jax
import jax.numpy as jnp
from jax import lax
from jax.experimental import pallas as pl
from jax.experimental.pallas import tpu as pltpu

F32 = jnp.float32
BF16 = jnp.bfloat16

LANES = 128
ROPE_THETA = 10000.0
LN_EPS = 1e-5
RMS_EPS = 1e-6
MAX_PAGES_PER_STEP = 16
SCORE_LOOKAHEAD = 3
VMEM_LIMIT = 56 * 1024 * 1024


def _cparams(*sem):
    return pltpu.CompilerParams(dimension_semantics=sem, vmem_limit_bytes=VMEM_LIMIT)


def _const_spec(shape):
    nd = len(shape)
    return pl.BlockSpec(shape, lambda *_: (0,) * nd, pipeline_mode=pl.Buffered(1))


def _gelu(x):
    return 0.5 * x * (1.0 + lax.erf(x * math.sqrt(0.5)))


def _layernorm(x, g, b):
    mu = jnp.mean(x, -1, keepdims=True)
    d = x - mu
    var = jnp.mean(d * d, -1, keepdims=True)
    return d * lax.rsqrt(var + LN_EPS) * g + b


def _gate_kernel(x_ref, wu_ref, wv_ref, lng_ref, lnb_ref, ws_ref, bs_ref, a_ref, *v_out,
                 seq_len, chunk, groups):
    x = x_ref[...].astype(BF16)
    u = _gelu(jnp.dot(x, wu_ref[...], preferred_element_type=F32))
    v = _layernorm(_gelu(jnp.dot(x, wv_ref[...], preferred_element_type=F32)),
                   lng_ref[...], lnb_ref[...])
    if v_out:
        v_out[0][...] = v
    vb = v.astype(BF16)
    tm, d = v.shape
    gw = d // groups
    row = lax.broadcasted_iota(jnp.int32, (chunk, chunk), 0)
    col = lax.broadcasted_iota(jnp.int32, (chunk, chunk), 1)
    mask = (col <= row) & ((row // seq_len) == (col // seq_len))
    for g in range(groups):
        w = jnp.where(mask, ws_ref[g], 0.0).astype(BF16)
        for c in range(tm // chunk):
            rs, cs = slice(c * chunk, (c + 1) * chunk), slice(g * gw, (g + 1) * gw)
            mixed = jnp.dot(w, vb[rs, cs], preferred_element_type=F32) + bs_ref[g]
            a_ref[rs, cs] = (u[rs, cs] * mixed).astype(a_ref.dtype)


def _gate_call(x, wu, wv, lng, lnb, ws, bs, *, seq_len, want_v, tm):
    t, d = x.shape
    groups, chunk, _ = ws.shape
    gw = d // groups
    out_shape = [jax.ShapeDtypeStruct((t, d), BF16)]
    out_specs = [pl.BlockSpec((tm, d), lambda i: (i, 0))]
    if want_v:
        out_shape.append(jax.ShapeDtypeStruct((t, d), F32))
        out_specs.append(pl.BlockSpec((tm, d), lambda i: (i, 0)))
    return pl.pallas_call(
        functools.partial(_gate_kernel, seq_len=seq_len, chunk=chunk, groups=groups),
        grid=(t // tm,),
        in_specs=[
            pl.BlockSpec((tm, d), lambda i: (i, 0)),
            _const_spec((d, d)), _const_spec((d, d)),
            _const_spec((1, d)), _const_spec((1, d)),
            _const_spec((groups, chunk, chunk)), _const_spec((groups, chunk, gw)),
        ],
        out_specs=out_specs,
        out_shape=out_shape,
        compiler_params=_cparams("parallel"),
        name="gate",
    )(x, wu, wv, lng, lnb, ws, bs)


def _qkv_kernel(x_ref, w_ref, kvg_ref, cos_ref, sin_ref, wuk_ref,
                q_ref, ckv_ref, kr_ref, kvb_ref, *, n_heads, nope, rank, rope, scale):
    x = x_ref[...].astype(BF16)
    z = jnp.dot(x, w_ref[...], preferred_element_type=F32)
    tm = z.shape[0]
    nope_w = n_heads * nope
    c = z[:, nope_w:nope_w + rank]
    c = c * lax.rsqrt(jnp.mean(c * c, -1, keepdims=True) + RMS_EPS) * kvg_ref[...]
    ckv_ref[...] = c
    kvb_ref[:, :rank] = c.astype(BF16)

    off = nope_w + rank
    cos, sin = cos_ref[...], sin_ref[...]
    lane = lax.broadcasted_iota(jnp.int32, (tm, LANES), 1)
    first_half = (lane % rope) < (rope // 2)

    def rotate(xg):
        swapped = jnp.where(first_half, pltpu.roll(xg, LANES - rope // 2, 1),
                            pltpu.roll(xg, rope // 2, 1))
        return xg * cos + swapped * sin

    for k in range(n_heads // 2):
        pair = rotate(z[:, off + LANES * k: off + LANES * (k + 1)]) * scale
        for h, slot in ((2 * k, pair), (2 * k + 1, pltpu.roll(pair, rope, 1))):
            ql = jnp.dot(z[:, h * nope:(h + 1) * nope].astype(BF16), wuk_ref[h],
                         preferred_element_type=F32) * scale
            q_ref[h, :, :rank] = ql.astype(q_ref.dtype)
            q_ref[h, :, rank:] = slot.astype(q_ref.dtype)
    kr_off = off + LANES * (n_heads // 2)
    kr = rotate(z[:, kr_off:kr_off + LANES])
    kr_ref[...] = kr[:, :rope]
    kvb_ref[:, rank:] = kr.astype(BF16)


def _qkv_call(x, w, kvg, cos, sin, wuk, *, q_dtype, table_blocks, scale, tm):
    t, d = x.shape
    n_heads, nope, rank = wuk.shape
    rope = LANES // 2
    qw = rank + LANES
    ncols = w.shape[1]
    return pl.pallas_call(
        functools.partial(_qkv_kernel, n_heads=n_heads, nope=nope, rank=rank, rope=rope,
                          scale=scale),
        grid=(t // tm,),
        in_specs=[
            pl.BlockSpec((tm, d), lambda i: (i, 0)),
            _const_spec((d, ncols)), _const_spec((1, rank)),
            pl.BlockSpec((tm, LANES), lambda i: (i % table_blocks, 0)),
            pl.BlockSpec((tm, LANES), lambda i: (i % table_blocks, 0)),
            _const_spec((n_heads, nope, rank)),
        ],
        out_specs=[
            pl.BlockSpec((n_heads, tm, qw), lambda i: (0, i, 0)),
            pl.BlockSpec((tm, rank), lambda i: (i, 0)),
            pl.BlockSpec((tm, rope), lambda i: (i, 0)),
            pl.BlockSpec((tm, qw), lambda i: (i, 0)),
        ],
        out_shape=[
            jax.ShapeDtypeStruct((n_heads, t, qw), q_dtype),
            jax.ShapeDtypeStruct((t, rank), F32),
            jax.ShapeDtypeStruct((t, rope), F32),
            jax.ShapeDtypeStruct((t, qw), BF16),
        ],
        compiler_params=_cparams("parallel"),
        name="qkv",
    )(x, w, kvg, cos, sin, wuk)


def _lane_slabs(x):
    return [x[:, c:c + LANES] for c in range(0, x.shape[1], LANES)]


def _softmax_update(s, v, m_ref, l_ref, acc_ref, h):
    p, corr = _softmax_probs(s, m_ref, l_ref, h)
    _accumulate_pv(p, corr, v, acc_ref, h)


def _softmax_probs(s, m_ref, l_ref, h):
    slabs = _lane_slabs(s)
    m_prev = m_ref[h]
    m_new = jnp.maximum(m_prev, jnp.max(functools.reduce(jnp.maximum, slabs), -1, keepdims=True))
    corr = jnp.exp2(m_prev - m_new)
    p_slabs = [jnp.exp2(x - m_new) for x in slabs]
    l_ref[h] = l_ref[h] * corr + jnp.sum(functools.reduce(jnp.add, p_slabs), -1, keepdims=True)
    m_ref[h] = m_new
    return jnp.concatenate(p_slabs, 1).astype(BF16), corr


def _accumulate_pv(p, corr, v, acc_ref, h):
    pv = jnp.dot(p, v, preferred_element_type=F32)
    acc_ref[h] = jnp.concatenate(
        [a * corr + b for a, b in zip(_lane_slabs(acc_ref[h]), _lane_slabs(pv))], 1)


def _attn_epilogue(o_ref, wuv_ref, l_ref, acc_ref):
    n_heads, _, vd = wuv_ref.shape
    for h in range(n_heads):
        inv = 1.0 / l_ref[h]
        o_lat = jnp.concatenate([a * inv for a in _lane_slabs(acc_ref[h])], 1).astype(BF16)
        o_ref[:, h * vd:(h + 1) * vd] = jnp.dot(
            o_lat, wuv_ref[h], preferred_element_type=F32).astype(o_ref.dtype)


def _attn_prompt_kernel(q_ref, kv_ref, wuv_ref, o_ref, m_ref, l_ref, acc_ref, *, tk, rank):
    qi = pl.program_id(1)
    n_heads, tq, _ = q_ref.shape
    m_ref[...] = jnp.full(m_ref.shape, -jnp.inf, F32)
    l_ref[...] = jnp.zeros(l_ref.shape, F32)
    acc_ref[...] = jnp.zeros(acc_ref.shape, F32)

    def chunk(start, diagonal):
        k = kv_ref[pl.ds(start, tk), :]
        v = k[:, :rank]
        if diagonal:
            qpos = qi * tq + lax.broadcasted_iota(jnp.int32, (tq, tk), 0)
            visible = lax.broadcasted_iota(jnp.int32, (tq, tk), 1) + start <= qpos
        def scores(h):
            return lax.dot_general(q_ref[h], k, (((1,), (1,)), ((), ())),
                                   preferred_element_type=F32)

        s_q, p_q = {}, {}
        la = SCORE_LOOKAHEAD
        for h in range(n_heads + la + 1):
            if h < n_heads:
                s_q[h] = scores(h)
            if h >= la + 1:
                _accumulate_pv(*p_q.pop(h - la - 1), v, acc_ref, h - la - 1)
            if la <= h < n_heads + la:
                s = s_q.pop(h - la)
                if diagonal:
                    s = jnp.where(visible, s, -jnp.inf)
                p_q[h - la] = _softmax_probs(s, m_ref, l_ref, h - la)

    n_full = (qi * tq) // tk

    def body(c, carry):
        chunk(pl.multiple_of(c * tk, tk), False)
        return carry

    lax.fori_loop(0, n_full, body, 0)
    chunk(pl.multiple_of(n_full * tk, tk), True)
    _attn_epilogue(o_ref, wuv_ref, l_ref, acc_ref)


def _attn_prompt_call(q, kvb, wuv, *, batch, seq, tq, tk):
    n_heads, t, qw = q.shape
    _, rank, vd = wuv.shape
    nq = seq // tq
    assert tk % tq == 0 and seq % tk == 0
    return pl.pallas_call(
        functools.partial(_attn_prompt_kernel, tk=tk, rank=rank),
        grid=(batch, nq),
        in_specs=[
            pl.BlockSpec((n_heads, tq, qw), lambda b, i: (0, b * nq + i, 0)),
            pl.BlockSpec((seq, qw), lambda b, i: (b, 0)),
            _const_spec((n_heads, rank, vd)),
        ],
        out_specs=pl.BlockSpec((tq, n_heads * vd), lambda b, i: (b * nq + i, 0)),
        out_shape=jax.ShapeDtypeStruct((t, n_heads * vd), BF16),
        scratch_shapes=[
            pltpu.VMEM((n_heads, tq, LANES), F32), pltpu.VMEM((n_heads, tq, LANES), F32),
            pltpu.VMEM((n_heads, tq, rank), F32),
        ],
        compiler_params=_cparams("parallel", "parallel"),
        name="attn_prompt",
    )(q, kvb, wuv)


def _attn_sample_kernel(pt_ref, q_ref, cnew_ref, krnew_ref, wuv_ref, cache_c, cache_krt, o_ref,
                        cbuf, krbuf, sem, m_ref, l_ref, acc_ref, kc_ref, krc_ref,
                        *, layer, pages, rank, rope):
    b, j = pl.program_id(0), pl.program_id(1)
    nj = pl.num_programs(1)
    step = b * nj + j
    last = pl.num_programs(0) * nj - 1
    slot = step % 2
    n_heads, tn, qw = q_ref.shape
    rows = n_heads * tn
    page = cbuf.shape[2]

    def page_copies(bb, jj, sl):
        out = []
        for g in range(pages):
            pg = pt_ref[bb, jj * pages + g]
            out.append(pltpu.make_async_copy(cache_c.at[layer, pg], cbuf.at[sl, g], sem.at[0, sl]))
            out.append(pltpu.make_async_copy(cache_krt.at[layer, pg], krbuf.at[sl, g], sem.at[1, sl]))
        return out

    @pl.when(step == 0)
    def _():
        for cp in page_copies(b, j, slot):
            cp.start()

    @pl.when(step < last)
    def _():
        wrap = j + 1 == nj
        for cp in page_copies(jnp.where(wrap, b + 1, b), jnp.where(wrap, 0, j + 1), 1 - slot):
            cp.start()

    q = q_ref[...].reshape(rows, qw)
    q_lat = q[:, :rank].astype(BF16)
    q_rope = q[:, rank:rank + rope].astype(BF16)
    nt = (((1,), (1,)), ((), ()))

    @pl.when(j == 0)
    def _():
        m_ref[...] = jnp.full(m_ref.shape, -jnp.inf, F32)
        l_ref[...] = jnp.zeros(l_ref.shape, F32)
        acc_ref[...] = jnp.zeros(acc_ref.shape, F32)
        pad = page - tn
        kc = jnp.concatenate([cnew_ref[...], jnp.zeros((pad, rank), F32)], 0).astype(BF16)
        krc = jnp.concatenate([krnew_ref[...], jnp.zeros((pad, rope), F32)], 0).astype(BF16)
        s = (lax.dot_general(q_lat, kc, nt, preferred_element_type=F32)
             + lax.dot_general(q_rope, krc, nt, preferred_element_type=F32))
        tok = lax.broadcasted_iota(jnp.int32, (rows, page), 0) % tn
        key = lax.broadcasted_iota(jnp.int32, (rows, page), 1)
        _softmax_update(jnp.where(key <= tok, s, -jnp.inf), kc, m_ref, l_ref, acc_ref, 0)

    for cp in page_copies(b, j, slot):
        cp.wait()
    for g in range(pages):
        kc_ref[g * page:(g + 1) * page, :] = cbuf[slot, g].astype(BF16)
        krc_ref[:, g * page:(g + 1) * page] = krbuf[slot, g].astype(BF16)
    kc = kc_ref[...]
    s = (lax.dot_general(q_lat, kc, nt, preferred_element_type=F32)
         + jnp.dot(q_rope, krc_ref[...], preferred_element_type=F32))
    _softmax_update(s, kc, m_ref, l_ref, acc_ref, 0)

    @pl.when(j == nj - 1)
    def _():
        inv = 1.0 / l_ref[0]
        o_lat = jnp.concatenate([a * inv for a in _lane_slabs(acc_ref[0])], 1).astype(BF16)
        vd = wuv_ref.shape[2]
        for h in range(n_heads):
            o_ref[:, h * vd:(h + 1) * vd] = jnp.dot(
                o_lat[h * tn:(h + 1) * tn], wuv_ref[h], preferred_element_type=F32
            ).astype(o_ref.dtype)


def _attn_sample_call(page_table, q, c_new, kr_new, wuv, cache_c, cache_krt, *, layer, tn):
    n_heads, t, qw = q.shape
    _, rank, vd = wuv.shape
    rope = kr_new.shape[1]
    bd, n_pages = page_table.shape
    page = cache_c.shape[2]
    pages = math.gcd(n_pages, MAX_PAGES_PER_STEP)
    rows = n_heads * tn
    grid_spec = pltpu.PrefetchScalarGridSpec(
        num_scalar_prefetch=1,
        grid=(bd, n_pages // pages),
        in_specs=[
            pl.BlockSpec((n_heads, tn, qw), lambda b, j, pt: (0, b, 0)),
            pl.BlockSpec((tn, rank), lambda b, j, pt: (b, 0)),
            pl.BlockSpec((tn, rope), lambda b, j, pt: (b, 0)),
            pl.BlockSpec((n_heads, rank, vd), lambda b, j, pt: (0, 0, 0)),
            pl.BlockSpec(memory_space=pl.ANY),
            pl.BlockSpec(memory_space=pl.ANY),
        ],
        out_specs=pl.BlockSpec((tn, n_heads * vd), lambda b, j, pt: (b, 0)),
        scratch_shapes=[
            pltpu.VMEM((2, pages, page, rank), F32),
            pltpu.VMEM((2, pages, rope, page), F32),
            pltpu.SemaphoreType.DMA((2, 2)),
            pltpu.VMEM((1, rows, LANES), F32), pltpu.VMEM((1, rows, LANES), F32),
            pltpu.VMEM((1, rows, rank), F32),
            pltpu.VMEM((pages * page, rank), BF16), pltpu.VMEM((rope, pages * page), BF16),
        ],
    )
    return pl.pallas_call(
        functools.partial(_attn_sample_kernel, layer=layer, pages=pages, rank=rank, rope=rope),
        grid_spec=grid_spec,
        out_shape=jax.ShapeDtypeStruct((t, n_heads * vd), BF16),
        compiler_params=_cparams("arbitrary", "arbitrary"),
        name="attn_sample",
    )(page_table, q, c_new, kr_new, wuv, cache_c, cache_krt)


def _merge_kernel(x_ref, a_ref, o_ref, wg_ref, wo_ref, g_ref, b_ref, x1_ref, *, alpha):
    x = x_ref[...]
    d = x.shape[1]
    gates = jax.nn.sigmoid(jnp.dot(x.astype(BF16), wg_ref[...], preferred_element_type=F32))
    h = gates[:, :d] * a_ref[...].astype(F32) + gates[:, d:] * o_ref[...].astype(F32)
    t = jnp.dot(h.astype(BF16), wo_ref[...], preferred_element_type=F32)
    x1_ref[...] = _layernorm(alpha * x + t, g_ref[...], b_ref[...])


def _merge_call(x, a, o, wg, wo, g, b, *, alpha, tm):
    t, d = x.shape
    tok = pl.BlockSpec((tm, d), lambda i: (i, 0))
    return pl.pallas_call(
        functools.partial(_merge_kernel, alpha=alpha),
        grid=(t // tm,),
        in_specs=[tok, tok, tok, _const_spec((d, 2 * d)), _const_spec((d, d)),
                  _const_spec((1, d)), _const_spec((1, d))],
        out_specs=tok,
        out_shape=jax.ShapeDtypeStruct((t, d), F32),
        compiler_params=_cparams("parallel"),
        name="merge",
    )(x, a, o, wg, wo, g, b)


def _ffn_kernel(x1_ref, wup_ref, wdn_ref, g_ref, b_ref, y_ref, xb_ref, acc_ref, *, alpha):
    j = pl.program_id(1)

    @pl.when(j == 0)
    def _():
        xb_ref[...] = x1_ref[...].astype(BF16)
        acc_ref[...] = jnp.zeros(acc_ref.shape, F32)

    f = jnp.maximum(jnp.dot(xb_ref[...], wup_ref[...], preferred_element_type=F32), 0.0)
    acc_ref[...] += jnp.dot((f * f).astype(BF16), wdn_ref[...], preferred_element_type=F32)

    @pl.when(j == pl.num_programs(1) - 1)
    def _():
        y_ref[...] = _layernorm(alpha * x1_ref[...] + acc_ref[...], g_ref[...], b_ref[...])


def _ffn_call(x1, wup, wdn, g, b, *, alpha, tm, tf):
    t, d = x1.shape
    dff = wup.shape[1]
    return pl.pallas_call(
        functools.partial(_ffn_kernel, alpha=alpha),
        grid=(t // tm, dff // tf),
        in_specs=[
            pl.BlockSpec((tm, d), lambda i, j: (i, 0)),
            pl.BlockSpec((d, tf), lambda i, j: (0, j)),
            pl.BlockSpec((tf, d), lambda i, j: (j, 0)),
            _const_spec((1, d)), _const_spec((1, d)),
        ],
        out_specs=pl.BlockSpec((tm, d), lambda i, j: (i, 0)),
        out_shape=jax.ShapeDtypeStruct((t, d), F32),
        scratch_shapes=[pltpu.VMEM((tm, d), BF16), pltpu.VMEM((tm, d), F32)],
        compiler_params=_cparams("parallel", "arbitrary"),
        name="ffn",
    )(x1, wup, wdn, g, b)


def _rope_tables(pos, rope):
    inv = ROPE_THETA ** (-jnp.arange(0, rope, 2, dtype=F32) / rope)
    ang = pos.astype(F32)[:, None] * inv[None, :]
    cos, sin = jnp.cos(ang), jnp.sin(ang)
    reps = LANES // rope
    return (jnp.tile(jnp.concatenate([cos, cos], -1), (1, reps)),
            jnp.tile(jnp.concatenate([-sin, sin], -1), (1, reps)))


def _tile_rows(t, want):
    return want if t % want == 0 else t


def kernel(x_prompt, x_sample, cache_kv_latent, cache_k_rope, page_table, w_in, gmlp_ln_g, gmlp_ln_b, gmlp_ws, gmlp_bs, kv_norm_g, w_uk, w_uv, w_o, ln1_g, ln1_b, w_up, w_down, ln2_g, ln2_b):
    batch, seq, d = x_prompt.shape
    bd, tn, _ = x_sample.shape
    depth = w_in.shape[0]
    _, rank, n_heads, nope = w_uk.shape
    rope = cache_k_rope.shape[-1]
    dg = gmlp_ln_g.shape[-1]
    groups, chunk = gmlp_ws.shape[1], gmlp_ws.shape[2]
    page = cache_kv_latent.shape[2]
    assert nope == LANES and 2 * rope == LANES and rank % LANES == 0 and dg == d
    assert n_heads % 2 == 0 and chunk % tn == 0 and page >= tn
    alpha = (2 * depth) ** 0.25
    scale = (nope + rope) ** -0.5 * math.log2(math.e)
    cache_krt = jnp.swapaxes(cache_k_rope, 2, 3)
    q_dim = n_heads * (nope + rope)
    o_q, o_c, o_kr, o_g = 2 * dg, 2 * dg + q_dim, 2 * dg + q_dim + rank, 2 * dg + q_dim + rank + rope

    head_base = o_q + (nope + rope) * jnp.arange(n_heads)
    cols = jnp.concatenate([
        (head_base[:, None] + jnp.arange(nope)[None, :]).reshape(-1),
        o_c + jnp.arange(rank),
        (head_base[:, None] + nope + jnp.arange(rope)[None, :]).reshape(-1),
        o_kr + jnp.arange(rope),
    ])

    past = page_table.shape[1] * page
    cos_p, sin_p = _rope_tables(jnp.arange(seq), rope)
    tm_s = _tile_rows(bd * tn, 256)
    cos_s, sin_s = _rope_tables(past + jnp.arange(tn), rope)
    cos_s, sin_s = jnp.tile(cos_s, (tm_s // tn, 1)), jnp.tile(sin_s, (tm_s // tn, 1))

    hp = x_prompt.reshape(batch * seq, d)
    hs = x_sample.reshape(bd * tn, d)
    tm_p = _tile_rows(batch * seq, 256)
    outs = {k: [] for k in ("c_p", "kr_p", "c_s", "kr_s", "v_s")}
    for l in range(depth):
        wl = w_in[l]
        wu, wv = wl[:, :dg].astype(BF16), wl[:, dg:2 * dg].astype(BF16)
        wq = jnp.pad(wl[:, cols], ((0, 0), (0, LANES - rope))).astype(BF16)
        wg = wl[:, o_g:].astype(BF16)
        wuk = jnp.transpose(w_uk[l], (1, 2, 0)).astype(BF16)
        wuv = jnp.transpose(w_uv[l], (1, 0, 2)).astype(BF16)
        wo, wup, wdn = w_o[l].astype(BF16), w_up[l].astype(BF16), w_down[l].astype(BF16)
        lng, lnb = gmlp_ln_g[l][None], gmlp_ln_b[l][None]
        kvg = kv_norm_g[l][None]
        gw = dg // groups
        ws_p = gmlp_ws[l]
        bs_p = jnp.broadcast_to(gmlp_bs[l][:, :, None], (groups, chunk, gw))
        rep = chunk // tn
        ws_s = jnp.tile(gmlp_ws[l][:, :tn, :tn], (1, rep, rep))
        bs_s = jnp.broadcast_to(jnp.tile(gmlp_bs[l][:, :tn], (1, rep))[:, :, None], (groups, chunk, gw))

        a = _gate_call(hp, wu, wv, lng, lnb, ws_p, bs_p, seq_len=min(seq, chunk), want_v=False, tm=tm_p)[0]
        q, c_kv, k_r, kvb = _qkv_call(hp, wq, kvg, cos_p, sin_p, wuk, q_dtype=BF16,
                                      table_blocks=seq // tm_p, scale=scale, tm=tm_p)
        o = _attn_prompt_call(q, kvb, wuv, batch=batch, seq=seq, tq=LANES, tk=256)
        x1 = _merge_call(hp, a, o, wg, wo, ln1_g[l][None], ln1_b[l][None], alpha=alpha, tm=tm_p)
        hp = _ffn_call(x1, wup, wdn, ln2_g[l][None], ln2_b[l][None], alpha=alpha,
                       tm=_tile_rows(batch * seq, 512), tf=512)
        outs["c_p"].append(c_kv.reshape(batch, seq, rank))
        outs["kr_p"].append(k_r.reshape(batch, seq, rope))

        a, v = _gate_call(hs, wu, wv, lng, lnb, ws_s, bs_s, seq_len=tn, want_v=True, tm=tm_s)
        q, c_kv, k_r, _ = _qkv_call(hs, wq, kvg, cos_s, sin_s, wuk, q_dtype=F32,
                                    table_blocks=1, scale=scale, tm=tm_s)
        o = _attn_sample_call(page_table, q, c_kv, k_r, wuv, cache_kv_latent, cache_krt,
                              layer=l, tn=tn)
        x1 = _merge_call(hs, a, o, wg, wo, ln1_g[l][None], ln1_b[l][None], alpha=alpha, tm=tm_s)
        hs = _ffn_call(x1, wup, wdn, ln2_g[l][None], ln2_b[l][None], alpha=alpha,
                       tm=_tile_rows(bd * tn, 512), tf=512)
        outs["c_s"].append(c_kv.reshape(bd, tn, rank))
        outs["kr_s"].append(k_r.reshape(bd, tn, rope))
        outs["v_s"].append(v.reshape(bd, tn, dg))

    return (hp.reshape(batch, seq, d), hs.reshape(bd, tn, d),
            jnp.stack(outs["c_p"]), jnp.stack(outs["kr_p"]),
            jnp.stack(outs["c_s"]), jnp.stack(outs["kr_s"]), jnp.stack(outs["v_s"]))
```

```python
import functools
import math

import jax
import jax.numpy as jnp
from jax import lax
from jax.experimental import pallas as pl
from jax.experimental.pallas import tpu as pltpu

F32 = jnp.float32
BF16 = jnp.bfloat16

LANES = 128
ROPE_THETA = 10000.0
LN_EPS = 1e-5
RMS_EPS = 1e-6
MAX_PAGES_PER_STEP = 16
SAMPLE_PAGES_PER_SOFTMAX = 4
SCORE_LOOKAHEAD = 3
VMEM_LIMIT = 56 * 1024 * 1024


def _cparams(*sem):
    return pltpu.CompilerParams(dimension_semantics=sem, vmem_limit_bytes=VMEM_LIMIT)


def _const_spec(shape):
    nd = len(shape)
    return pl.BlockSpec(shape, lambda *_: (0,) * nd, pipeline_mode=pl.Buffered(1))


def _gelu(x):
    return 0.5 * x * (1.0 + lax.erf(x * math.sqrt(0.5)))


def _layernorm(x, g, b):
    mu = jnp.mean(x, -1, keepdims=True)
    d = x - mu
    var = jnp.mean(d * d, -1, keepdims=True)
    return d * lax.rsqrt(var + LN_EPS) * g + b


def _dot_nt(x, w):
    return lax.dot_general(x, w, (((1,), (1,)), ((), ())), preferred_element_type=F32)


def _gate_kernel(x_ref, wuv_ref, lng_ref, lnb_ref, ws_ref, bs_ref, a_ref, *v_out,
                 seq_len, chunk, groups):
    x = x_ref[...].astype(BF16)
    d = x.shape[1]
    u = _gelu(_dot_nt(x, wuv_ref[:d, :]))
    v = _layernorm(_gelu(_dot_nt(x, wuv_ref[d:, :])), lng_ref[...], lnb_ref[...])
    if v_out:
        v_out[0][...] = v
    vb = v.astype(BF16)
    tm, d = v.shape
    gw = d // groups
    row = lax.broadcasted_iota(jnp.int32, (chunk, chunk), 0)
    col = lax.broadcasted_iota(jnp.int32, (chunk, chunk), 1)
    mask = (col <= row) & ((row // seq_len) == (col // seq_len))
    for g in range(groups):
        w = jnp.where(mask, ws_ref[g], 0.0).astype(BF16)
        for c in range(tm // chunk):
            rs, cs = slice(c * chunk, (c + 1) * chunk), slice(g * gw, (g + 1) * gw)
            mixed = jnp.dot(w, vb[rs, cs], preferred_element_type=F32) + bs_ref[g]
            a_ref[rs, cs] = (u[rs, cs] * mixed).astype(a_ref.dtype)


def _gate_call(x, w_t, lng, lnb, ws, bs, *, seq_len, want_v, tm):
    t, d = x.shape
    assert tm % ws.shape[1] == 0
    groups, chunk, _ = ws.shape
    gw = d // groups
    out_shape = [jax.ShapeDtypeStruct((t, d), BF16)]
    out_specs = [pl.BlockSpec((tm, d), lambda i: (i, 0))]
    if want_v:
        out_shape.append(jax.ShapeDtypeStruct((t, d), F32))
        out_specs.append(pl.BlockSpec((tm, d), lambda i: (i, 0)))
    return pl.pallas_call(
        functools.partial(_gate_kernel, seq_len=seq_len, chunk=chunk, groups=groups),
        grid=(t // tm,),
        in_specs=[
            pl.BlockSpec((tm, d), lambda i: (i, 0)),
            _const_spec((2 * d, d)),
            _const_spec((1, d)), _const_spec((1, d)),
            _const_spec((groups, chunk, chunk)), _const_spec((groups, chunk, gw)),
        ],
        out_specs=out_specs,
        out_shape=out_shape,
        compiler_params=_cparams("parallel"),
        name="gate",
    )(x, w_t, lng, lnb, ws, bs)


def _qkv_kernel(x_ref, w_ref, kvg_ref, cos_ref, sin_ref, wuk_ref,
                q_ref, ckv_ref, kr_ref, kvb_ref, *, n_heads, nope, rank, rope, scale):
    x = x_ref[...].astype(BF16)
    z = _dot_nt(x, w_ref[...])
    tm = z.shape[0]
    nope_w = n_heads * nope
    c = z[:, nope_w:nope_w + rank]
    c = c * lax.rsqrt(jnp.mean(c * c, -1, keepdims=True) + RMS_EPS) * kvg_ref[...]
    ckv_ref[...] = c
    kvb_ref[:, :rank] = c.astype(BF16)

    off = nope_w + rank
    cos, sin = cos_ref[...], sin_ref[...]
    lane = lax.broadcasted_iota(jnp.int32, (tm, LANES), 1)
    first_half = (lane % rope) < (rope // 2)

    def rotate(xg):
        swapped = jnp.where(first_half, pltpu.roll(xg, LANES - rope // 2, 1),
                            pltpu.roll(xg, rope // 2, 1))
        return xg * cos + swapped * sin

    for k in range(n_heads // 2):
        pair = rotate(z[:, off + LANES * k: off + LANES * (k + 1)]) * scale
        for h, slot in ((2 * k, pair), (2 * k + 1, pltpu.roll(pair, rope, 1))):
            ql = jnp.dot(z[:, h * nope:(h + 1) * nope].astype(BF16), wuk_ref[h],
                         preferred_element_type=F32) * scale
            q_ref[h, :, :rank] = ql.astype(q_ref.dtype)
            q_ref[h, :, rank:] = slot.astype(q_ref.dtype)
    kr_off = off + LANES * (n_heads // 2)
    kr = rotate(z[:, kr_off:kr_off + LANES])
    kr_ref[...] = kr[:, :rope]
    kvb_ref[:, rank:] = kr.astype(BF16)


def _qkv_call(x, w, kvg, cos, sin, wuk, *, q_dtype, table_blocks, scale, tm):
    t, d = x.shape
    n_heads, nope, rank = wuk.shape
    rope = LANES // 2
    qw = rank + LANES
    ncols = w.shape[0]
    return pl.pallas_call(
        functools.partial(_qkv_kernel, n_heads=n_heads, nope=nope, rank=rank, rope=rope,
                          scale=scale),
        grid=(t // tm,),
        in_specs=[
            pl.BlockSpec((tm, d), lambda i: (i, 0)),
            _const_spec((ncols, d)), _const_spec((1, rank)),
            pl.BlockSpec((tm, LANES), lambda i: (i % table_blocks, 0)),
            pl.BlockSpec((tm, LANES), lambda i: (i % table_blocks, 0)),
            _const_spec((n_heads, nope, rank)),
        ],
        out_specs=[
            pl.BlockSpec((n_heads, tm, qw), lambda i: (0, i, 0)),
            pl.BlockSpec((tm, rank), lambda i: (i, 0)),
            pl.BlockSpec((tm, rope), lambda i: (i, 0)),
            pl.BlockSpec((tm, qw), lambda i: (i, 0)),
        ],
        out_shape=[
            jax.ShapeDtypeStruct((n_heads, t, qw), q_dtype),
            jax.ShapeDtypeStruct((t, rank), F32),
            jax.ShapeDtypeStruct((t, rope), F32),
            jax.ShapeDtypeStruct((t, qw), BF16),
        ],
        compiler_params=_cparams("parallel"),
        name="qkv",
    )(x, w, kvg, cos, sin, wuk)


def _lane_slabs(x):
    return [x[:, c:c + LANES] for c in range(0, x.shape[1], LANES)]


def _softmax_update(s, v, m_ref, l_ref, acc_ref, h):
    p, corr = _softmax_probs(s, m_ref, l_ref, h)
    _accumulate_pv(p, corr, v, acc_ref, h)


def _softmax_probs(s, m_ref, l_ref, h):
    slabs = _lane_slabs(s)
    m_prev = m_ref[h]
    m_new = jnp.maximum(m_prev, jnp.max(functools.reduce(jnp.maximum, slabs), -1, keepdims=True))
    corr = jnp.exp2(m_prev - m_new)
    p_slabs = [jnp.exp2(x - m_new) for x in slabs]
    l_ref[h] = l_ref[h] * corr + jnp.sum(functools.reduce(jnp.add, p_slabs), -1, keepdims=True)
    m_ref[h] = m_new
    return jnp.concatenate(p_slabs, 1).astype(BF16), corr


def _accumulate_pv(p, corr, v, acc_ref, h):
    pv = jnp.dot(p, v, preferred_element_type=F32)
    acc_ref[h] = jnp.concatenate(
        [a * corr + b for a, b in zip(_lane_slabs(acc_ref[h]), _lane_slabs(pv))], 1)


def _attn_epilogue(o_ref, wuv_ref, l_ref, acc_ref):
    n_heads, _, vd = wuv_ref.shape
    for h in range(n_heads):
        inv = 1.0 / l_ref[h]
        o_lat = jnp.concatenate([a * inv for a in _lane_slabs(acc_ref[h])], 1).astype(BF16)
        o_ref[:, h * vd:(h + 1) * vd] = jnp.dot(
            o_lat, wuv_ref[h], preferred_element_type=F32).astype(o_ref.dtype)


def _attn_prompt_kernel(q_ref, kv_ref, wuv_ref, o_ref, m_ref, l_ref, acc_ref, *, tk, rank):
    qi = pl.program_id(1)
    n_heads, tq, _ = q_ref.shape
    m_ref[...] = jnp.full(m_ref.shape, -jnp.inf, F32)
    l_ref[...] = jnp.zeros(l_ref.shape, F32)
    acc_ref[...] = jnp.zeros(acc_ref.shape, F32)

    def chunk(start, diagonal):
        k = kv_ref[pl.ds(start, tk), :]
        v = k[:, :rank]
        if diagonal:
            qpos = qi * tq + lax.broadcasted_iota(jnp.int32, (tq, tk), 0)
            visible = lax.broadcasted_iota(jnp.int32, (tq, tk), 1) + start <= qpos
        def scores(h):
            return lax.dot_general(q_ref[h], k, (((1,), (1,)), ((), ())),
                                   preferred_element_type=F32)

        s_q, p_q = {}, {}
        la = SCORE_LOOKAHEAD
        for h in range(n_heads + la + 1):
            if h < n_heads:
                s_q[h] = scores(h)
            if h >= la + 1:
                _accumulate_pv(*p_q.pop(h - la - 1), v, acc_ref, h - la - 1)
            if la <= h < n_heads + la:
                s = s_q.pop(h - la)
                if diagonal:
                    s = jnp.where(visible, s, -jnp.inf)
                p_q[h - la] = _softmax_probs(s, m_ref, l_ref, h - la)

    n_full = (qi * tq) // tk

    def body(c, carry):
        chunk(pl.multiple_of(c * tk, tk), False)
        return carry

    lax.fori_loop(0, n_full, body, 0)
    chunk(pl.multiple_of(n_full * tk, tk), True)
    _attn_epilogue(o_ref, wuv_ref, l_ref, acc_ref)


def _attn_prompt_call(q, kvb, wuv, *, batch, seq, tq, tk):
    n_heads, t, qw = q.shape
    _, rank, vd = wuv.shape
    nq = seq // tq
    assert tk % tq == 0 and seq % tk == 0
    return pl.pallas_call(
        functools.partial(_attn_prompt_kernel, tk=tk, rank=rank),
        grid=(batch, nq),
        in_specs=[
            pl.BlockSpec((n_heads, tq, qw), lambda b, i: (0, b * nq + i, 0)),
            pl.BlockSpec((seq, qw), lambda b, i: (b, 0)),
            _const_spec((n_heads, rank, vd)),
        ],
        out_specs=pl.BlockSpec((tq, n_heads * vd), lambda b, i: (b * nq + i, 0)),
        out_shape=jax.ShapeDtypeStruct((t, n_heads * vd), BF16),
        scratch_shapes=[
            pltpu.VMEM((n_heads, tq, LANES), F32), pltpu.VMEM((n_heads, tq, LANES), F32),
            pltpu.VMEM((n_heads, tq, rank), F32),
        ],
        compiler_params=_cparams("parallel", "parallel"),
        name="attn_prompt",
    )(q, kvb, wuv)


def _attn_sample_kernel(pt_ref, q_ref, cnew_ref, krnew_ref, wuv_ref, cache_c, cache_krt, o_ref,
                        cbuf, krbuf, sem, m_ref, l_ref, acc_ref, kc_ref, krc_ref,
                        *, layer, pages, rank, rope):
    b, j = pl.program_id(0), pl.program_id(1)
    nj = pl.num_programs(1)
    step = b * nj + j
    last = pl.num_programs(0) * nj - 1
    slot = step % 2
    n_heads, tn, qw = q_ref.shape
    rows = n_heads * tn
    page = cbuf.shape[2]

    def page_copies(bb, jj, sl):
        out = []
        for g in range(pages):
            pg = pt_ref[bb, jj * pages + g]
            out.append(pltpu.make_async_copy(cache_c.at[layer, pg], cbuf.at[sl, g], sem.at[0, sl]))
            out.append(pltpu.make_async_copy(cache_krt.at[layer, pg], krbuf.at[sl, g], sem.at[1, sl]))
        return out

    def start_all(copies):
        for i, cp in enumerate(copies):
            cp.start(priority=(i // 2) % 2)

    @pl.when(step == 0)
    def _():
        start_all(page_copies(b, j, slot))

    @pl.when(step < last)
    def _():
        wrap = j + 1 == nj
        start_all(page_copies(jnp.where(wrap, b + 1, b), jnp.where(wrap, 0, j + 1), 1 - slot))

    q = q_ref[...].reshape(rows, qw)
    q_lat = q[:, :rank].astype(BF16)
    q_rope = q[:, rank:rank + rope].astype(BF16)
    nt = (((1,), (1,)), ((), ()))

    @pl.when(j == 0)
    def _():
        m_ref[...] = jnp.full(m_ref.shape, -jnp.inf, F32)
        l_ref[...] = jnp.zeros(l_ref.shape, F32)
        acc_ref[...] = jnp.zeros(acc_ref.shape, F32)
        pad = page - tn
        kc = jnp.concatenate([cnew_ref[...], jnp.zeros((pad, rank), F32)], 0).astype(BF16)
        krc = jnp.concatenate([krnew_ref[...], jnp.zeros((pad, rope), F32)], 0).astype(BF16)
        s = (lax.dot_general(q_lat, kc, nt, preferred_element_type=F32)
             + lax.dot_general(q_rope, krc, nt, preferred_element_type=F32))
        tok = lax.broadcasted_iota(jnp.int32, (rows, page), 0) % tn
        key = lax.broadcasted_iota(jnp.int32, (rows, page), 1)
        _softmax_update(jnp.where(key <= tok, s, -jnp.inf), kc, m_ref, l_ref, acc_ref, 0)

    for cp in page_copies(b, j, slot):
        cp.wait()

    gp = SAMPLE_PAGES_PER_SOFTMAX
    rows_c = gp * page

    def scores(c):
        for g in range(c * gp, (c + 1) * gp):
            kc_ref[g * page:(g + 1) * page, :] = cbuf[slot, g].astype(BF16)
            krc_ref[:, g * page:(g + 1) * page] = krbuf[slot, g].astype(BF16)
        return (lax.dot_general(q_lat, kc_ref[c * rows_c:(c + 1) * rows_c, :], nt,
                                preferred_element_type=F32)
                + jnp.dot(q_rope, krc_ref[:, c * rows_c:(c + 1) * rows_c],
                          preferred_element_type=F32))

    n_sub = pages // gp
    s_q, p_q = {}, {}
    for c in range(n_sub + 2):
        if c < n_sub:
            s_q[c] = scores(c)
        if c >= 2:
            _accumulate_pv(*p_q.pop(c - 2), kc_ref[(c - 2) * rows_c:(c - 1) * rows_c, :],
                           acc_ref, 0)
        if 1 <= c <= n_sub:
            p_q[c - 1] = _softmax_probs(s_q.pop(c - 1), m_ref, l_ref, 0)

    @pl.when(j == nj - 1)
    def _():
        inv = 1.0 / l_ref[0]
        o_lat = jnp.concatenate([a * inv for a in _lane_slabs(acc_ref[0])], 1).astype(BF16)
        vd = wuv_ref.shape[2]
        for h in range(n_heads):
            o_ref[:, h * vd:(h + 1) * vd] = jnp.dot(
                o_lat[h * tn:(h + 1) * tn], wuv_ref[h], preferred_element_type=F32
            ).astype(o_ref.dtype)


def _attn_sample_call(page_table, q, c_new, kr_new, wuv, cache_c, cache_krt, *, layer, tn):
    n_heads, t, qw = q.shape
    _, rank, vd = wuv.shape
    rope = kr_new.shape[1]
    bd, n_pages = page_table.shape
    page = cache_c.shape[2]
    pages = math.gcd(n_pages, MAX_PAGES_PER_STEP)
    rows = n_heads * tn
    grid_spec = pltpu.PrefetchScalarGridSpec(
        num_scalar_prefetch=1,
        grid=(bd, n_pages // pages),
        in_specs=[
            pl.BlockSpec((n_heads, tn, qw), lambda b, j, pt: (0, b, 0)),
            pl.BlockSpec((tn, rank), lambda b, j, pt: (b, 0)),
            pl.BlockSpec((tn, rope), lambda b, j, pt: (b, 0)),
            pl.BlockSpec((n_heads, rank, vd), lambda b, j, pt: (0, 0, 0)),
            pl.BlockSpec(memory_space=pl.ANY),
            pl.BlockSpec(memory_space=pl.ANY),
        ],
        out_specs=pl.BlockSpec((tn, n_heads * vd), lambda b, j, pt: (b, 0)),
        scratch_shapes=[
            pltpu.VMEM((2, pages, page, rank), F32),
            pltpu.VMEM((2, pages, rope, page), F32),
            pltpu.SemaphoreType.DMA((2, 2)),
            pltpu.VMEM((1, rows, LANES), F32), pltpu.VMEM((1, rows, LANES), F32),
            pltpu.VMEM((1, rows, rank), F32),
            pltpu.VMEM((pages * page, rank), BF16), pltpu.VMEM((rope, pages * page), BF16),
        ],
    )
    return pl.pallas_call(
        functools.partial(_attn_sample_kernel, layer=layer, pages=pages, rank=rank, rope=rope),
        grid_spec=grid_spec,
        out_shape=jax.ShapeDtypeStruct((t, n_heads * vd), BF16),
        compiler_params=_cparams("arbitrary", "arbitrary"),
        name="attn_sample",
    )(page_table, q, c_new, kr_new, wuv, cache_c, cache_krt)


def _merge_kernel(x_ref, a_ref, o_ref, wg_ref, wo_ref, g_ref, b_ref, x1_ref, *, alpha):
    x = x_ref[...]
    d = x.shape[1]
    gates = jax.nn.sigmoid(_dot_nt(x.astype(BF16), wg_ref[...]))
    h = gates[:, :d] * a_ref[...].astype(F32) + gates[:, d:] * o_ref[...].astype(F32)
    t = jnp.dot(h.astype(BF16), wo_ref[...], preferred_element_type=F32)
    x1_ref[...] = _layernorm(alpha * x + t, g_ref[...], b_ref[...])


def _merge_call(x, a, o, wg, wo, g, b, *, alpha, tm):
    t, d = x.shape
    tok = pl.BlockSpec((tm, d), lambda i: (i, 0))
    return pl.pallas_call(
        functools.partial(_merge_kernel, alpha=alpha),
        grid=(t // tm,),
        in_specs=[tok, tok, tok, _const_spec((2 * d, d)), _const_spec((d, d)),
                  _const_spec((1, d)), _const_spec((1, d))],
        out_specs=tok,
        out_shape=jax.ShapeDtypeStruct((t, d), F32),
        compiler_params=_cparams("parallel"),
        name="merge",
    )(x, a, o, wg, wo, g, b)


def _ffn_kernel(x1_ref, wup_ref, wdn_ref, g_ref, b_ref, y_ref, xb_ref, acc_ref, *, alpha):
    j = pl.program_id(1)

    @pl.when(j == 0)
    def _():
        xb_ref[...] = x1_ref[...].astype(BF16)
        acc_ref[...] = jnp.zeros(acc_ref.shape, F32)

    f = jnp.maximum(jnp.dot(xb_ref[...], wup_ref[...], preferred_element_type=F32), 0.0)
    acc_ref[...] += jnp.dot((f * f).astype(BF16), wdn_ref[...], preferred_element_type=F32)

    @pl.when(j == pl.num_programs(1) - 1)
    def _():
        y_ref[...] = _layernorm(alpha * x1_ref[...] + acc_ref[...], g_ref[...], b_ref[...])


def _ffn_call(x1, wup, wdn, g, b, *, alpha, tm, tf):
    t, d = x1.shape
    dff = wup.shape[1]
    return pl.pallas_call(
        functools.partial(_ffn_kernel, alpha=alpha),
        grid=(t // tm, dff // tf),
        in_specs=[
            pl.BlockSpec((tm, d), lambda i, j: (i, 0)),
            pl.BlockSpec((d, tf), lambda i, j: (0, j)),
            pl.BlockSpec((tf, d), lambda i, j: (j, 0)),
            _const_spec((1, d)), _const_spec((1, d)),
        ],
        out_specs=pl.BlockSpec((tm, d), lambda i, j: (i, 0)),
        out_shape=jax.ShapeDtypeStruct((t, d), F32),
        scratch_shapes=[pltpu.VMEM((tm, d), BF16), pltpu.VMEM((tm, d), F32)],
        compiler_params=_cparams("parallel", "arbitrary"),
        name="ffn",
    )(x1, wup, wdn, g, b)


def _rope_tables(pos, rope):
    inv = ROPE_THETA ** (-jnp.arange(0, rope, 2, dtype=F32) / rope)
    ang = pos.astype(F32)[:, None] * inv[None, :]
    cos, sin = jnp.cos(ang), jnp.sin(ang)
    reps = LANES // rope
    return (jnp.tile(jnp.concatenate([cos, cos], -1), (1, reps)),
            jnp.tile(jnp.concatenate([-sin, sin], -1), (1, reps)))


def _tile_rows(t, want):
    return want if t % want == 0 else t


def kernel(x_prompt, x_sample, cache_kv_latent, cache_k_rope, page_table, w_in, gmlp_ln_g, gmlp_ln_b, gmlp_ws, gmlp_bs, kv_norm_g, w_uk, w_uv, w_o, ln1_g, ln1_b, w_up, w_down, ln2_g, ln2_b):
    batch, seq, d = x_prompt.shape
    bd, tn, _ = x_sample.shape
    depth = w_in.shape[0]
    _, rank, n_heads, nope = w_uk.shape
    rope = cache_k_rope.shape[-1]
    dg = gmlp_ln_g.shape[-1]
    groups, chunk = gmlp_ws.shape[1], gmlp_ws.shape[2]
    page = cache_kv_latent.shape[2]
    assert nope == LANES and 2 * rope == LANES and rank % LANES == 0 and dg == d
    assert n_heads % 2 == 0 and chunk % tn == 0 and page >= tn
    alpha = (2 * depth) ** 0.25
    scale = (nope + rope) ** -0.5 * math.log2(math.e)
    cache_krt = jnp.swapaxes(cache_k_rope, 2, 3)
    q_dim = n_heads * (nope + rope)
    o_q, o_c, o_kr, o_g = 2 * dg, 2 * dg + q_dim, 2 * dg + q_dim + rank, 2 * dg + q_dim + rank + rope

    head_base = o_q + (nope + rope) * jnp.arange(n_heads)
    cols = jnp.concatenate([
        (head_base[:, None] + jnp.arange(nope)[None, :]).reshape(-1),
        o_c + jnp.arange(rank),
        (head_base[:, None] + nope + jnp.arange(rope)[None, :]).reshape(-1),
        o_kr + jnp.arange(rope),
    ])

    past = page_table.shape[1] * page
    cos_p, sin_p = _rope_tables(jnp.arange(seq), rope)
    tm_s = _tile_rows(bd * tn, 256)
    cos_s, sin_s = _rope_tables(past + jnp.arange(tn), rope)
    cos_s, sin_s = jnp.tile(cos_s, (tm_s // tn, 1)), jnp.tile(sin_s, (tm_s // tn, 1))

    hp = x_prompt.reshape(batch * seq, d)
    hs = x_sample.reshape(bd * tn, d)
    tm_p = _tile_rows(batch * seq, 256)
    outs = {k: [] for k in ("c_p", "kr_p", "c_s", "kr_s", "v_s")}
    for l in range(depth):
        w_t = jnp.swapaxes(w_in[l], 0, 1).astype(BF16)
        wq = jnp.pad(w_t[cols], ((0, LANES - rope), (0, 0)))
        wg = w_t[o_g:]
        wuk = jnp.transpose(w_uk[l], (1, 2, 0)).astype(BF16)
        wuv = jnp.transpose(w_uv[l], (1, 0, 2)).astype(BF16)
        wo, wup, wdn = w_o[l].astype(BF16), w_up[l].astype(BF16), w_down[l].astype(BF16)
        lng, lnb = gmlp_ln_g[l][None], gmlp_ln_b[l][None]
        kvg = kv_norm_g[l][None]
        gw = dg // groups
        ws_p = gmlp_ws[l]
        bs_p = jnp.broadcast_to(gmlp_bs[l][:, :, None], (groups, chunk, gw))
        rep = chunk // tn
        ws_s = jnp.tile(gmlp_ws[l][:, :tn, :tn], (1, rep, rep))
        bs_s = jnp.broadcast_to(jnp.tile(gmlp_bs[l][:, :tn], (1, rep))[:, :, None], (groups, chunk, gw))

        a = _gate_call(hp, w_t, lng, lnb, ws_p, bs_p, seq_len=min(seq, chunk), want_v=False, tm=tm_p)[0]
        q, c_kv, k_r, kvb = _qkv_call(hp, wq, kvg, cos_p, sin_p, wuk, q_dtype=BF16,
                                      table_blocks=seq // tm_p, scale=scale, tm=tm_p)
        o = _attn_prompt_call(q, kvb, wuv, batch=batch, seq=seq, tq=LANES, tk=256)
        x1 = _merge_call(hp, a, o, wg, wo, ln1_g[l][None], ln1_b[l][None], alpha=alpha, tm=tm_p)
        hp = _ffn_call(x1, wup, wdn, ln2_g[l][None], ln2_b[l][None], alpha=alpha,
                       tm=_tile_rows(batch * seq, 512), tf=512)
        outs["c_p"].append(c_kv.reshape(batch, seq, rank))
        outs["kr_p"].append(k_r.reshape(batch, seq, rope))

        a, v = _gate_call(hs, w_t, lng, lnb, ws_s, bs_s, seq_len=tn, want_v=True, tm=tm_s)
        q, c_kv, k_r, _ = _qkv_call(hs, wq, kvg, cos_s, sin_s, wuk, q_dtype=F32,
                                    table_blocks=1, scale=scale, tm=tm_s)
        o = _attn_sample_call(page_table, q, c_kv, k_r, wuv, cache_kv_latent, cache_krt,
                              layer=l, tn=tn)
        x1 = _merge_call(hs, a, o, wg, wo, ln1_g[l][None], ln1_b[l][None], alpha=alpha, tm=tm_s)
        hs = _ffn_call(x1, wup, wdn, ln2_g[l][None], ln2_b[l][None], alpha=alpha,
                       tm=_tile_rows(bd * tn, 512), tf=512)
        outs["c_s"].append(c_kv.reshape(bd, tn, rank))
        outs["kr_s"].append(k_r.reshape(bd, tn, rope))
        outs["v_s"].append(v.reshape(bd, tn, dg))

    return (hp.reshape(batch, seq, d), hs.reshape(bd, tn, d),
            jnp.stack(outs["c_p"]), jnp.stack(outs["kr_p"]),
            jnp.stack(outs["c_s"]), jnp.stack(outs["kr_s"]), jnp.stack(outs["v_s"]))
```

```python
import functools
import math

import jax
import jax.numpy as jnp
from jax import lax
from jax.experimental import pallas as pl
from jax.experimental.pallas import tpu as pltpu

F32 = jnp.float32
BF16 = jnp.bfloat16

LANES = 128
ROPE_THETA = 10000.0
LN_EPS = 1e-5
RMS_EPS = 1e-6
MAX_PAGES_PER_STEP = 32
SAMPLE_PAGES_PER_SOFTMAX = 4
SCORE_LOOKAHEAD = 3
VMEM_LIMIT = 56 * 1024 * 1024


def _cparams(*sem):
    return pltpu.CompilerParams(dimension_semantics=sem, vmem_limit_bytes=VMEM_LIMIT)


def _const_spec(shape):
    nd = len(shape)
    return pl.BlockSpec(shape, lambda *_: (0,) * nd, pipeline_mode=pl.Buffered(1))


def _gelu(x):
    return 0.5 * x * (1.0 + lax.erf(x * math.sqrt(0.5)))


def _layernorm(x, g, b):
    mu = jnp.mean(x, -1, keepdims=True)
    d = x - mu
    var = jnp.mean(d * d, -1, keepdims=True)
    return d * lax.rsqrt(var + LN_EPS) * g + b


def _dot_nt(x, w):
    return lax.dot_general(x, w, (((1,), (1,)), ((), ())), preferred_element_type=F32)


def _gate_kernel(x_ref, wuv_ref, lng_ref, lnb_ref, ws_ref, bs_ref, a_ref, *v_out,
                 seq_len, chunk, groups):
    x = x_ref[...].astype(BF16)
    d = x.shape[1]
    u = _gelu(_dot_nt(x, wuv_ref[:d, :]))
    v = _layernorm(_gelu(_dot_nt(x, wuv_ref[d:, :])), lng_ref[...], lnb_ref[...])
    if v_out:
        v_out[0][...] = v
    vb = v.astype(BF16)
    tm, d = v.shape
    gw = d // groups
    row = lax.broadcasted_iota(jnp.int32, (chunk, chunk), 0)
    col = lax.broadcasted_iota(jnp.int32, (chunk, chunk), 1)
    mask = (col <= row) & ((row // seq_len) == (col // seq_len))
    for g in range(groups):
        w = jnp.where(mask, ws_ref[g], 0.0).astype(BF16)
        for c in range(tm // chunk):
            rs, cs = slice(c * chunk, (c + 1) * chunk), slice(g * gw, (g + 1) * gw)
            mixed = jnp.dot(w, vb[rs, cs], preferred_element_type=F32) + bs_ref[g]
            a_ref[rs, cs] = (u[rs, cs] * mixed).astype(a_ref.dtype)


def _gate_call(x, w_t, lng, lnb, ws, bs, *, seq_len, want_v, tm):
    t, d = x.shape
    assert tm % ws.shape[1] == 0
    groups, chunk, _ = ws.shape
    gw = d // groups
    out_shape = [jax.ShapeDtypeStruct((t, d), BF16)]
    out_specs = [pl.BlockSpec((tm, d), lambda i: (i, 0))]
    if want_v:
        out_shape.append(jax.ShapeDtypeStruct((t, d), F32))
        out_specs.append(pl.BlockSpec((tm, d), lambda i: (i, 0)))
    return pl.pallas_call(
        functools.partial(_gate_kernel, seq_len=seq_len, chunk=chunk, groups=groups),
        grid=(t // tm,),
        in_specs=[
            pl.BlockSpec((tm, d), lambda i: (i, 0)),
            _const_spec((2 * d, d)),
            _const_spec((1, d)), _const_spec((1, d)),
            _const_spec((groups, chunk, chunk)), _const_spec((groups, chunk, gw)),
        ],
        out_specs=out_specs,
        out_shape=out_shape,
        compiler_params=_cparams("parallel"),
        name="gate",
    )(x, w_t, lng, lnb, ws, bs)


def _qkv_kernel(x_ref, w_ref, kvg_ref, cos_ref, sin_ref, wk_ref, *rest,
                n_heads, nope, rank, rope, scale, absorb):
    if absorb:
        q_ref, ckv_ref, kr_ref, kvb_ref = rest
    else:
        wv_ref, q_ref, ckv_ref, kr_ref, k_ref, v_ref = rest
    x = x_ref[...].astype(BF16)
    z = _dot_nt(x, w_ref[...])
    tm = z.shape[0]
    nope_w = n_heads * nope
    c = z[:, nope_w:nope_w + rank]
    c = c * lax.rsqrt(jnp.mean(c * c, -1, keepdims=True) + RMS_EPS) * kvg_ref[...]
    ckv_ref[...] = c
    cb = c.astype(BF16)

    off = nope_w + rank
    cos, sin = cos_ref[...], sin_ref[...]
    lane = lax.broadcasted_iota(jnp.int32, (tm, LANES), 1)
    first_half = (lane % rope) < (rope // 2)

    def rotate(xg):
        swapped = jnp.where(first_half, pltpu.roll(xg, LANES - rope // 2, 1),
                            pltpu.roll(xg, rope // 2, 1))
        return xg * cos + swapped * sin

    kr_off = off + LANES * (n_heads // 2)
    kr = rotate(z[:, kr_off:kr_off + LANES])
    kr_ref[...] = kr[:, :rope]
    krb = kr.astype(BF16)
    if absorb:
        kvb_ref[:, :rank] = cb
        kvb_ref[:, rank:] = krb
        qn_w = rank
    else:
        kn = jnp.dot(cb, wk_ref[...], preferred_element_type=F32)
        vn = jnp.dot(cb, wv_ref[...], preferred_element_type=F32)
        vd = vn.shape[1] // n_heads
        qn_w = nope

    for k in range(n_heads // 2):
        pair = rotate(z[:, off + LANES * k: off + LANES * (k + 1)]) * scale
        for h, slot in ((2 * k, pair), (2 * k + 1, pltpu.roll(pair, rope, 1))):
            qn = z[:, h * nope:(h + 1) * nope]
            if absorb:
                qn = jnp.dot(qn.astype(BF16), wk_ref[h], preferred_element_type=F32)
            else:
                k_ref[h, :, :nope] = kn[:, h * nope:(h + 1) * nope].astype(BF16)
                k_ref[h, :, nope:] = krb
                v_ref[h] = vn[:, h * vd:(h + 1) * vd].astype(BF16)
            q_ref[h, :, :qn_w] = (qn * scale).astype(q_ref.dtype)
            q_ref[h, :, qn_w:] = slot.astype(q_ref.dtype)


def _qkv_call(x, w, kvg, cos, sin, wk, wv=None, *, q_dtype, table_blocks, scale, tm):
    t, d = x.shape
    absorb = wv is None
    rope = LANES // 2
    ncols = w.shape[0]
    if absorb:
        n_heads, nope, rank = wk.shape
        qw = rank + LANES
        w_specs = [_const_spec(wk.shape)]
        extra_specs = [pl.BlockSpec((tm, qw), lambda i: (i, 0))]
        extra_shapes = [jax.ShapeDtypeStruct((t, qw), BF16)]
        weights = (wk,)
    else:
        rank = wk.shape[0]
        nope = LANES
        n_heads = wk.shape[1] // nope
        vd = wv.shape[1] // n_heads
        qw = nope + LANES
        w_specs = [_const_spec(wk.shape), _const_spec(wv.shape)]
        extra_specs = [pl.BlockSpec((n_heads, tm, qw), lambda i: (0, i, 0)),
                       pl.BlockSpec((n_heads, tm, vd), lambda i: (0, i, 0))]
        extra_shapes = [jax.ShapeDtypeStruct((n_heads, t, qw), BF16),
                        jax.ShapeDtypeStruct((n_heads, t, vd), BF16)]
        weights = (wk, wv)
    return pl.pallas_call(
        functools.partial(_qkv_kernel, n_heads=n_heads, nope=nope, rank=rank, rope=rope,
                          scale=scale, absorb=absorb),
        grid=(t // tm,),
        in_specs=[
            pl.BlockSpec((tm, d), lambda i: (i, 0)),
            _const_spec((ncols, d)), _const_spec((1, rank)),
            pl.BlockSpec((tm, LANES), lambda i: (i % table_blocks, 0)),
            pl.BlockSpec((tm, LANES), lambda i: (i % table_blocks, 0)),
        ] + w_specs,
        out_specs=[
            pl.BlockSpec((n_heads, tm, qw), lambda i: (0, i, 0)),
            pl.BlockSpec((tm, rank), lambda i: (i, 0)),
            pl.BlockSpec((tm, rope), lambda i: (i, 0)),
        ] + extra_specs,
        out_shape=[
            jax.ShapeDtypeStruct((n_heads, t, qw), q_dtype),
            jax.ShapeDtypeStruct((t, rank), F32),
            jax.ShapeDtypeStruct((t, rope), F32),
        ] + extra_shapes,
        compiler_params=_cparams("parallel"),
        name="qkv",
    )(x, w, kvg, cos, sin, *weights)


def _lane_slabs(x):
    return [x[:, c:c + LANES] for c in range(0, x.shape[1], LANES)]


def _softmax_update(s, v, m_ref, l_ref, acc_ref, h):
    p, corr = _softmax_probs(s, m_ref, l_ref, h)
    _accumulate_pv(p, corr, v, acc_ref, h)


def _softmax_probs(s, m_ref, l_ref, h):
    slabs = _lane_slabs(s)
    m_prev = m_ref[h]
    m_new = jnp.maximum(m_prev, jnp.max(functools.reduce(jnp.maximum, slabs), -1, keepdims=True))
    corr = jnp.exp2(m_prev - m_new)
    p_slabs = [jnp.exp2(x - m_new) for x in slabs]
    l_ref[h] = l_ref[h] * corr + jnp.sum(functools.reduce(jnp.add, p_slabs), -1, keepdims=True)
    m_ref[h] = m_new
    return jnp.concatenate(p_slabs, 1).astype(BF16), corr


def _accumulate_pv(p, corr, v, acc_ref, h):
    pv = jnp.dot(p, v, preferred_element_type=F32)
    acc_ref[h] = jnp.concatenate(
        [a * corr + b for a, b in zip(_lane_slabs(acc_ref[h]), _lane_slabs(pv))], 1)


def _attn_prompt_kernel(q_ref, k_ref, v_ref, o_ref, m_ref, l_ref, acc_ref, *, tk):
    qi = pl.program_id(1)
    n_heads, tq, _ = q_ref.shape
    vd = v_ref.shape[2]
    m_ref[...] = jnp.full(m_ref.shape, -jnp.inf, F32)
    l_ref[...] = jnp.zeros(l_ref.shape, F32)
    acc_ref[...] = jnp.zeros(acc_ref.shape, F32)

    def chunk(start, diagonal):
        if diagonal:
            qpos = qi * tq + lax.broadcasted_iota(jnp.int32, (tq, tk), 0)
            visible = lax.broadcasted_iota(jnp.int32, (tq, tk), 1) + start <= qpos

        def scores(h):
            return _dot_nt(q_ref[h], k_ref[h, pl.ds(start, tk), :])

        s_q, p_q = {}, {}
        la = SCORE_LOOKAHEAD
        for h in range(n_heads + la + 1):
            if h < n_heads:
                s_q[h] = scores(h)
            if h >= la + 1:
                hp = h - la - 1
                _accumulate_pv(*p_q.pop(hp), v_ref[hp, pl.ds(start, tk), :], acc_ref, hp)
            if la <= h < n_heads + la:
                s = s_q.pop(h - la)
                if diagonal:
                    s = jnp.where(visible, s, -jnp.inf)
                p_q[h - la] = _softmax_probs(s, m_ref, l_ref, h - la)

    n_full = (qi * tq) // tk

    def body(c, carry):
        chunk(pl.multiple_of(c * tk, tk), False)
        return carry

    lax.fori_loop(0, n_full, body, 0)
    chunk(pl.multiple_of(n_full * tk, tk), True)
    for h in range(n_heads):
        o_ref[:, h * vd:(h + 1) * vd] = (acc_ref[h] * (1.0 / l_ref[h])).astype(o_ref.dtype)


def _attn_prompt_call(q, k, v, *, batch, seq, tq, tk):
    n_heads, t, qw = q.shape
    vd = v.shape[2]
    nq = seq // tq
    assert tk % tq == 0 and seq % tk == 0 and vd == LANES
    return pl.pallas_call(
        functools.partial(_attn_prompt_kernel, tk=tk),
        grid=(batch, nq),
        in_specs=[
            pl.BlockSpec((n_heads, tq, qw), lambda b, i: (0, b * nq + i, 0)),
            pl.BlockSpec((n_heads, seq, qw), lambda b, i: (0, b, 0), pipeline_mode=pl.Buffered(1)),
            pl.BlockSpec((n_heads, seq, vd), lambda b, i: (0, b, 0), pipeline_mode=pl.Buffered(1)),
        ],
        out_specs=pl.BlockSpec((tq, n_heads * vd), lambda b, i: (b * nq + i, 0)),
        out_shape=jax.ShapeDtypeStruct((t, n_heads * vd), BF16),
        scratch_shapes=[
            pltpu.VMEM((n_heads, tq, LANES), F32), pltpu.VMEM((n_heads, tq, LANES), F32),
            pltpu.VMEM((n_heads, tq, vd), F32),
        ],
        compiler_params=_cparams("parallel", "arbitrary"),
        name="attn_prompt",
    )(q, k, v)


def _attn_sample_kernel(pt_ref, q_ref, cnew_ref, krnew_ref, wuv_ref, cache_c, cache_krt, o_ref,
                        cbuf, krbuf, sem, m_ref, l_ref, acc_ref, kc_ref, krc_ref,
                        *, layer, pages, rank, rope):
    b, j = pl.program_id(0), pl.program_id(1)
    nj = pl.num_programs(1)
    step = b * nj + j
    last = pl.num_programs(0) * nj - 1
    slot = step % 2
    n_heads, tn, qw = q_ref.shape
    rows = n_heads * tn
    page = cbuf.shape[2]

    def page_copies(bb, jj, sl):
        out = []
        for g in range(pages):
            pg = pt_ref[bb, jj * pages + g]
            out.append(pltpu.make_async_copy(cache_c.at[layer, pg], cbuf.at[sl, g], sem.at[0, sl]))
            out.append(pltpu.make_async_copy(cache_krt.at[layer, pg], krbuf.at[sl, g], sem.at[1, sl]))
        return out

    def start_all(copies):
        for i, cp in enumerate(copies):
            cp.start(priority=(i // 2) % 2)

    @pl.when(step == 0)
    def _():
        start_all(page_copies(b, j, slot))

    @pl.when(step < last)
    def _():
        wrap = j + 1 == nj
        start_all(page_copies(jnp.where(wrap, b + 1, b), jnp.where(wrap, 0, j + 1), 1 - slot))

    q = q_ref[...].reshape(rows, qw)
    q_lat = q[:, :rank].astype(BF16)
    q_rope = q[:, rank:rank + rope].astype(BF16)
    nt = (((1,), (1,)), ((), ()))

    @pl.when(j == 0)
    def _():
        m_ref[...] = jnp.full(m_ref.shape, -jnp.inf, F32)
        l_ref[...] = jnp.zeros(l_ref.shape, F32)
        acc_ref[...] = jnp.zeros(acc_ref.shape, F32)
        pad = page - tn
        kc = jnp.concatenate([cnew_ref[...], jnp.zeros((pad, rank), F32)], 0).astype(BF16)
        krc = jnp.concatenate([krnew_ref[...], jnp.zeros((pad, rope), F32)], 0).astype(BF16)
        s = (lax.dot_general(q_lat, kc, nt, preferred_element_type=F32)
             + lax.dot_general(q_rope, krc, nt, preferred_element_type=F32))
        tok = lax.broadcasted_iota(jnp.int32, (rows, page), 0) % tn
        key = lax.broadcasted_iota(jnp.int32, (rows, page), 1)
        _softmax_update(jnp.where(key <= tok, s, -jnp.inf), kc, m_ref, l_ref, acc_ref, 0)

    for cp in page_copies(b, j, slot):
        cp.wait()

    gp = SAMPLE_PAGES_PER_SOFTMAX
    rows_c = gp * page

    def scores(c):
        for g in range(c * gp, (c + 1) * gp):
            kc_ref[g * page:(g + 1) * page, :] = cbuf[slot, g].astype(BF16)
            krc_ref[:, g * page:(g + 1) * page] = krbuf[slot, g].astype(BF16)
        return (lax.dot_general(q_lat, kc_ref[c * rows_c:(c + 1) * rows_c, :], nt,
                                preferred_element_type=F32)
                + jnp.dot(q_rope, krc_ref[:, c * rows_c:(c + 1) * rows_c],
                          preferred_element_type=F32))

    n_sub = pages // gp
    s_q, p_q = {}, {}
    for c in range(n_sub + 2):
        if c < n_sub:
            s_q[c] = scores(c)
        if c >= 2:
            _accumulate_pv(*p_q.pop(c - 2), kc_ref[(c - 2) * rows_c:(c - 1) * rows_c, :],
                           acc_ref, 0)
        if 1 <= c <= n_sub:
            p_q[c - 1] = _softmax_probs(s_q.pop(c - 1), m_ref, l_ref, 0)

    @pl.when(j == nj - 1)
    def _():
        inv = 1.0 / l_ref[0]
        o_lat = jnp.concatenate([a * inv for a in _lane_slabs(acc_ref[0])], 1).astype(BF16)
        vd = wuv_ref.shape[2]
        for h in range(n_heads):
            o_ref[:, h * vd:(h + 1) * vd] = jnp.dot(
                o_lat[h * tn:(h + 1) * tn], wuv_ref[h], preferred_element_type=F32
            ).astype(o_ref.dtype)


def _attn_sample_call(page_table, q, c_new, kr_new, wuv, cache_c, cache_krt, *, layer, tn):
    n_heads, t, qw = q.shape
    _, rank, vd = wuv.shape
    rope = kr_new.shape[1]
    bd, n_pages = page_table.shape
    page = cache_c.shape[2]
    pages = math.gcd(n_pages, MAX_PAGES_PER_STEP)
    assert pages % SAMPLE_PAGES_PER_SOFTMAX == 0
    rows = n_heads * tn
    grid_spec = pltpu.PrefetchScalarGridSpec(
        num_scalar_prefetch=1,
        grid=(bd, n_pages // pages),
        in_specs=[
            pl.BlockSpec((n_heads, tn, qw), lambda b, j, pt: (0, b, 0)),
            pl.BlockSpec((tn, rank), lambda b, j, pt: (b, 0)),
            pl.BlockSpec((tn, rope), lambda b, j, pt: (b, 0)),
            pl.BlockSpec((n_heads, rank, vd), lambda b, j, pt: (0, 0, 0)),
            pl.BlockSpec(memory_space=pl.ANY),
            pl.BlockSpec(memory_space=pl.ANY),
        ],
        out_specs=pl.BlockSpec((tn, n_heads * vd), lambda b, j, pt: (b, 0)),
        scratch_shapes=[
            pltpu.VMEM((2, pages, page, rank), F32),
            pltpu.VMEM((2, pages, rope, page), F32),
            pltpu.SemaphoreType.DMA((2, 2)),
            pltpu.VMEM((1, rows, LANES), F32), pltpu.VMEM((1, rows, LANES), F32),
            pltpu.VMEM((1, rows, rank), F32),
            pltpu.VMEM((pages * page, rank), BF16), pltpu.VMEM((rope, pages * page), BF16),
        ],
    )
    return pl.pallas_call(
        functools.partial(_attn_sample_kernel, layer=layer, pages=pages, rank=rank, rope=rope),
        grid_spec=grid_spec,
        out_shape=jax.ShapeDtypeStruct((t, n_heads * vd), BF16),
        compiler_params=_cparams("arbitrary", "arbitrary"),
        name="attn_sample",
    )(page_table, q, c_new, kr_new, wuv, cache_c, cache_krt)


def _merge_kernel(x_ref, a_ref, o_ref, wg_ref, wo_ref, g_ref, b_ref, x1_ref, *, alpha):
    x = x_ref[...]
    d = x.shape[1]
    gates = jax.nn.sigmoid(_dot_nt(x.astype(BF16), wg_ref[...]))
    h = gates[:, :d] * a_ref[...].astype(F32) + gates[:, d:] * o_ref[...].astype(F32)
    t = jnp.dot(h.astype(BF16), wo_ref[...], preferred_element_type=F32)
    x1_ref[...] = _layernorm(alpha * x + t, g_ref[...], b_ref[...])


def _merge_call(x, a, o, wg, wo, g, b, *, alpha, tm):
    t, d = x.shape
    tok = pl.BlockSpec((tm, d), lambda i: (i, 0))
    return pl.pallas_call(
        functools.partial(_merge_kernel, alpha=alpha),
        grid=(t // tm,),
        in_specs=[tok, tok, tok, _const_spec((2 * d, d)), _const_spec((d, d)),
                  _const_spec((1, d)), _const_spec((1, d))],
        out_specs=tok,
        out_shape=jax.ShapeDtypeStruct((t, d), F32),
        compiler_params=_cparams("parallel"),
        name="merge",
    )(x, a, o, wg, wo, g, b)


def _ffn_kernel(x1_ref, wup_ref, wdn_ref, g_ref, b_ref, y_ref, xb_ref, acc_ref, *, alpha):
    j = pl.program_id(1)

    @pl.when(j == 0)
    def _():
        xb_ref[...] = x1_ref[...].astype(BF16)
        acc_ref[...] = jnp.zeros(acc_ref.shape, F32)

    f = jnp.maximum(jnp.dot(xb_ref[...], wup_ref[...], preferred_element_type=F32), 0.0)
    acc_ref[...] += jnp.dot((f * f).astype(BF16), wdn_ref[...], preferred_element_type=F32)

    @pl.when(j == pl.num_programs(1) - 1)
    def _():
        y_ref[...] = _layernorm(alpha * x1_ref[...] + acc_ref[...], g_ref[...], b_ref[...])


def _ffn_call(x1, wup, wdn, g, b, *, alpha, tm, tf):
    t, d = x1.shape
    dff = wup.shape[1]
    return pl.pallas_call(
        functools.partial(_ffn_kernel, alpha=alpha),
        grid=(t // tm, dff // tf),
        in_specs=[
            pl.BlockSpec((tm, d), lambda i, j: (i, 0)),
            pl.BlockSpec((d, tf), lambda i, j: (0, j)),
            pl.BlockSpec((tf, d), lambda i, j: (j, 0)),
            _const_spec((1, d)), _const_spec((1, d)),
        ],
        out_specs=pl.BlockSpec((tm, d), lambda i, j: (i, 0)),
        out_shape=jax.ShapeDtypeStruct((t, d), F32),
        scratch_shapes=[pltpu.VMEM((tm, d), BF16), pltpu.VMEM((tm, d), F32)],
        compiler_params=_cparams("parallel", "arbitrary"),
        name="ffn",
    )(x1, wup, wdn, g, b)


def _rope_tables(pos, rope):
    inv = ROPE_THETA ** (-jnp.arange(0, rope, 2, dtype=F32) / rope)
    ang = pos.astype(F32)[:, None] * inv[None, :]
    cos, sin = jnp.cos(ang), jnp.sin(ang)
    reps = LANES // rope
    return (jnp.tile(jnp.concatenate([cos, cos], -1), (1, reps)),
            jnp.tile(jnp.concatenate([-sin, sin], -1), (1, reps)))


def _tile_rows(t, want):
    return want if t % want == 0 else t


def kernel(x_prompt, x_sample, cache_kv_latent, cache_k_rope, page_table, w_in, gmlp_ln_g, gmlp_ln_b, gmlp_ws, gmlp_bs, kv_norm_g, w_uk, w_uv, w_o, ln1_g, ln1_b, w_up, w_down, ln2_g, ln2_b):
    batch, seq, d = x_prompt.shape
    bd, tn, _ = x_sample.shape
    depth = w_in.shape[0]
    _, rank, n_heads, nope = w_uk.shape
    vd = w_uv.shape[3]
    rope = cache_k_rope.shape[-1]
    dg = gmlp_ln_g.shape[-1]
    groups, chunk = gmlp_ws.shape[1], gmlp_ws.shape[2]
    page = cache_kv_latent.shape[2]
    assert nope == LANES and 2 * rope == LANES and rank % LANES == 0 and dg == d
    assert n_heads % 2 == 0 and chunk % tn == 0 and page >= tn
    alpha = (2 * depth) ** 0.25
    scale = (nope + rope) ** -0.5 * math.log2(math.e)
    cache_krt = jnp.swapaxes(cache_k_rope, 2, 3)
    q_dim = n_heads * (nope + rope)
    o_q, o_c, o_kr, o_g = 2 * dg, 2 * dg + q_dim, 2 * dg + q_dim + rank, 2 * dg + q_dim + rank + rope

    head_base = o_q + (nope + rope) * jnp.arange(n_heads)
    cols = jnp.concatenate([
        (head_base[:, None] + jnp.arange(nope)[None, :]).reshape(-1),
        o_c + jnp.arange(rank),
        (head_base[:, None] + nope + jnp.arange(rope)[None, :]).reshape(-1),
        o_kr + jnp.arange(rope),
    ])

    past = page_table.shape[1] * page
    cos_p, sin_p = _rope_tables(jnp.arange(seq), rope)
    tm_s = _tile_rows(bd * tn, 256)
    cos_s, sin_s = _rope_tables(past + jnp.arange(tn), rope)
    cos_s, sin_s = jnp.tile(cos_s, (tm_s // tn, 1)), jnp.tile(sin_s, (tm_s // tn, 1))

    hp = x_prompt.reshape(batch * seq, d)
    hs = x_sample.reshape(bd * tn, d)
    tm_p = _tile_rows(batch * seq, 256)
    outs = {k: [] for k in ("c_p", "kr_p", "c_s", "kr_s", "v_s")}
    for l in range(depth):
        w_t = jnp.swapaxes(w_in[l], 0, 1).astype(BF16)
        wq = jnp.pad(w_t[cols], ((0, LANES - rope), (0, 0)))
        wg = w_t[o_g:]
        wuk_h = jnp.transpose(w_uk[l], (1, 2, 0)).astype(BF16)
        wuv_h = jnp.transpose(w_uv[l], (1, 0, 2)).astype(BF16)
        wuk_all = w_uk[l].reshape(rank, n_heads * nope).astype(BF16)
        wuv_all = w_uv[l].reshape(rank, n_heads * vd).astype(BF16)
        wo, wup, wdn = w_o[l].astype(BF16), w_up[l].astype(BF16), w_down[l].astype(BF16)
        lng, lnb = gmlp_ln_g[l][None], gmlp_ln_b[l][None]
        kvg = kv_norm_g[l][None]
        gw = dg // groups
        ws_p = gmlp_ws[l]
        bs_p = jnp.broadcast_to(gmlp_bs[l][:, :, None], (groups, chunk, gw))
        rep = chunk // tn
        ws_s = jnp.tile(gmlp_ws[l][:, :tn, :tn], (1, rep, rep))
        bs_s = jnp.broadcast_to(jnp.tile(gmlp_bs[l][:, :tn], (1, rep))[:, :, None], (groups, chunk, gw))

        a = _gate_call(hp, w_t, lng, lnb, ws_p, bs_p, seq_len=min(seq, chunk), want_v=False, tm=tm_p)[0]
        q, c_kv, k_r, k_h, v_h = _qkv_call(hp, wq, kvg, cos_p, sin_p, wuk_all, wuv_all, q_dtype=BF16,
                                           table_blocks=seq // tm_p, scale=scale, tm=tm_p)
        o = _attn_prompt_call(q, k_h, v_h, batch=batch, seq=seq, tq=LANES, tk=256)
        x1 = _merge_call(hp, a, o, wg, wo, ln1_g[l][None], ln1_b[l][None], alpha=alpha, tm=tm_p)
        hp = _ffn_call(x1, wup, wdn, ln2_g[l][None], ln2_b[l][None], alpha=alpha,
                       tm=_tile_rows(batch * seq, 512), tf=512)
        outs["c_p"].append(c_kv.reshape(batch, seq, rank))
        outs["kr_p"].append(k_r.reshape(batch, seq, rope))

        a, v = _gate_call(hs, w_t, lng, lnb, ws_s, bs_s, seq_len=tn, want_v=True, tm=tm_s)
        q, c_kv, k_r, _ = _qkv_call(hs, wq, kvg, cos_s, sin_s, wuk_h, q_dtype=F32,
                                    table_blocks=1, scale=scale, tm=tm_s)
        o = _attn_sample_call(page_table, q, c_kv, k_r, wuv_h, cache_kv_latent, cache_krt,
                              layer=l, tn=tn)
        x1 = _merge_call(hs, a, o, wg, wo, ln1_g[l][None], ln1_b[l][None], alpha=alpha, tm=tm_s)
        hs = _ffn_call(x1, wup, wdn, ln2_g[l][None], ln2_b[l][None], alpha=alpha,
                       tm=_tile_rows(bd * tn, 512), tf=512)
        outs["c_s"].append(c_kv.reshape(bd, tn, rank))
        outs["kr_s"].append(k_r.reshape(bd, tn, rope))
        outs["v_s"].append(v.reshape(bd, tn, dg))

    return (hp.reshape(batch, seq, d), hs.reshape(bd, tn, d),
            jnp.stack(outs["c_p"]), jnp.stack(outs["kr_p"]),
            jnp.stack(outs["c_s"]), jnp.stack(outs["kr_s"]), jnp.stack(outs["v_s"]))
```

```python
import functools
import math

import jax
import jax.numpy as jnp
from jax import lax
from jax.experimental import pallas as pl
from jax.experimental.pallas import tpu as pltpu

F32 = jnp.float32
BF16 = jnp.bfloat16

LANES = 128
ROPE_THETA = 10000.0
LN_EPS = 1e-5
RMS_EPS = 1e-6
MAX_PAGES_PER_STEP = 64
SAMPLE_PAGES_PER_SOFTMAX = 4
SCORE_LOOKAHEAD = 3
VMEM_LIMIT = 56 * 1024 * 1024


def _cparams(*sem):
    return pltpu.CompilerParams(dimension_semantics=sem, vmem_limit_bytes=VMEM_LIMIT)


def _const_spec(shape):
    nd = len(shape)
    return pl.BlockSpec(shape, lambda *_: (0,) * nd, pipeline_mode=pl.Buffered(1))


def _gelu(x):
    return 0.5 * x * (1.0 + lax.erf(x * math.sqrt(0.5)))


def _layernorm(x, g, b):
    mu = jnp.mean(x, -1, keepdims=True)
    d = x - mu
    var = jnp.mean(d * d, -1, keepdims=True)
    return d * lax.rsqrt(var + LN_EPS) * g + b


def _dot_nt(x, w):
    return lax.dot_general(x, w, (((1,), (1,)), ((), ())), preferred_element_type=F32)


def _gate_kernel(x_ref, wuv_ref, lng_ref, lnb_ref, ws_ref, bs_ref, a_ref, *v_out,
                 seq_len, chunk, groups):
    x = x_ref[...].astype(BF16)
    d = x.shape[1]
    u = _gelu(_dot_nt(x, wuv_ref[:d, :]))
    v = _layernorm(_gelu(_dot_nt(x, wuv_ref[d:, :])), lng_ref[...], lnb_ref[...])
    if v_out:
        v_out[0][...] = v
    vb = v.astype(BF16)
    tm, d = v.shape
    gw = d // groups
    row = lax.broadcasted_iota(jnp.int32, (chunk, chunk), 0)
    col = lax.broadcasted_iota(jnp.int32, (chunk, chunk), 1)
    mask = (col <= row) & ((row // seq_len) == (col // seq_len))
    for g in range(groups):
        w = jnp.where(mask, ws_ref[g], 0.0).astype(BF16)
        for c in range(tm // chunk):
            rs, cs = slice(c * chunk, (c + 1) * chunk), slice(g * gw, (g + 1) * gw)
            mixed = jnp.dot(w, vb[rs, cs], preferred_element_type=F32) + bs_ref[g]
            a_ref[rs, cs] = (u[rs, cs] * mixed).astype(a_ref.dtype)


def _gate_call(x, w_t, lng, lnb, ws, bs, *, seq_len, want_v, tm):
    t, d = x.shape
    assert tm % ws.shape[1] == 0
    groups, chunk, _ = ws.shape
    gw = d // groups
    out_shape = [jax.ShapeDtypeStruct((t, d), BF16)]
    out_specs = [pl.BlockSpec((tm, d), lambda i: (i, 0))]
    if want_v:
        out_shape.append(jax.ShapeDtypeStruct((t, d), F32))
        out_specs.append(pl.BlockSpec((tm, d), lambda i: (i, 0)))
    return pl.pallas_call(
        functools.partial(_gate_kernel, seq_len=seq_len, chunk=chunk, groups=groups),
        grid=(t // tm,),
        in_specs=[
            pl.BlockSpec((tm, d), lambda i: (i, 0)),
            _const_spec((2 * d, d)),
            _const_spec((1, d)), _const_spec((1, d)),
            _const_spec((groups, chunk, chunk)), _const_spec((groups, chunk, gw)),
        ],
        out_specs=out_specs,
        out_shape=out_shape,
        compiler_params=_cparams("parallel"),
        name="gate",
    )(x, w_t, lng, lnb, ws, bs)


def _qkv_kernel(x_ref, w_ref, kvg_ref, cos_ref, sin_ref, wk_ref, *rest,
                n_heads, nope, rank, rope, scale, absorb):
    if absorb:
        q_ref, ckv_ref, kr_ref, kvb_ref = rest
    else:
        wv_ref, q_ref, ckv_ref, kr_ref, k_ref, v_ref = rest
    x = x_ref[...].astype(BF16)
    z = _dot_nt(x, w_ref[...])
    tm = z.shape[0]
    nope_w = n_heads * nope
    c = z[:, nope_w:nope_w + rank]
    c = c * lax.rsqrt(jnp.mean(c * c, -1, keepdims=True) + RMS_EPS) * kvg_ref[...]
    ckv_ref[...] = c
    cb = c.astype(BF16)

    off = nope_w + rank
    cos, sin = cos_ref[...], sin_ref[...]
    lane = lax.broadcasted_iota(jnp.int32, (tm, LANES), 1)
    first_half = (lane % rope) < (rope // 2)

    def rotate(xg):
        swapped = jnp.where(first_half, pltpu.roll(xg, LANES - rope // 2, 1),
                            pltpu.roll(xg, rope // 2, 1))
        return xg * cos + swapped * sin

    kr_off = off + LANES * (n_heads // 2)
    kr = rotate(z[:, kr_off:kr_off + LANES])
    kr_ref[...] = kr[:, :rope]
    krb = kr.astype(BF16)
    if absorb:
        kvb_ref[:, :rank] = cb
        kvb_ref[:, rank:] = krb
        qn_w = rank
    else:
        kn = jnp.dot(cb, wk_ref[...], preferred_element_type=F32)
        vn = jnp.dot(cb, wv_ref[...], preferred_element_type=F32)
        vd = vn.shape[1] // n_heads
        qn_w = nope

    for k in range(n_heads // 2):
        pair = rotate(z[:, off + LANES * k: off + LANES * (k + 1)]) * scale
        for h, slot in ((2 * k, pair), (2 * k + 1, pltpu.roll(pair, rope, 1))):
            qn = z[:, h * nope:(h + 1) * nope]
            if absorb:
                qn = jnp.dot(qn.astype(BF16), wk_ref[h], preferred_element_type=F32)
            else:
                k_ref[h, :, :nope] = kn[:, h * nope:(h + 1) * nope].astype(BF16)
                k_ref[h, :, nope:] = krb
                v_ref[h] = vn[:, h * vd:(h + 1) * vd].astype(BF16)
            q_ref[h, :, :qn_w] = (qn * scale).astype(q_ref.dtype)
            q_ref[h, :, qn_w:] = slot.astype(q_ref.dtype)


def _qkv_call(x, w, kvg, cos, sin, wk, wv=None, *, q_dtype, table_blocks, scale, tm):
    t, d = x.shape
    absorb = wv is None
    rope = LANES // 2
    ncols = w.shape[0]
    if absorb:
        n_heads, nope, rank = wk.shape
        qw = rank + LANES
        w_specs = [_const_spec(wk.shape)]
        extra_specs = [pl.BlockSpec((tm, qw), lambda i: (i, 0))]
        extra_shapes = [jax.ShapeDtypeStruct((t, qw), BF16)]
        weights = (wk,)
    else:
        rank = wk.shape[0]
        nope = LANES
        n_heads = wk.shape[1] // nope
        vd = wv.shape[1] // n_heads
        qw = nope + LANES
        w_specs = [_const_spec(wk.shape), _const_spec(wv.shape)]
        extra_specs = [pl.BlockSpec((n_heads, tm, qw), lambda i: (0, i, 0)),
                       pl.BlockSpec((n_heads, tm, vd), lambda i: (0, i, 0))]
        extra_shapes = [jax.ShapeDtypeStruct((n_heads, t, qw), BF16),
                        jax.ShapeDtypeStruct((n_heads, t, vd), BF16)]
        weights = (wk, wv)
    return pl.pallas_call(
        functools.partial(_qkv_kernel, n_heads=n_heads, nope=nope, rank=rank, rope=rope,
                          scale=scale, absorb=absorb),
        grid=(t // tm,),
        in_specs=[
            pl.BlockSpec((tm, d), lambda i: (i, 0)),
            _const_spec((ncols, d)), _const_spec((1, rank)),
            pl.BlockSpec((tm, LANES), lambda i: (i % table_blocks, 0)),
            pl.BlockSpec((tm, LANES), lambda i: (i % table_blocks, 0)),
        ] + w_specs,
        out_specs=[
            pl.BlockSpec((n_heads, tm, qw), lambda i: (0, i, 0)),
            pl.BlockSpec((tm, rank), lambda i: (i, 0)),
            pl.BlockSpec((tm, rope), lambda i: (i, 0)),
        ] + extra_specs,
        out_shape=[
            jax.ShapeDtypeStruct((n_heads, t, qw), q_dtype),
            jax.ShapeDtypeStruct((t, rank), F32),
            jax.ShapeDtypeStruct((t, rope), F32),
        ] + extra_shapes,
        compiler_params=_cparams("parallel"),
        name="qkv",
    )(x, w, kvg, cos, sin, *weights)


def _lane_slabs(x):
    return [x[:, c:c + LANES] for c in range(0, x.shape[1], LANES)]


def _softmax_update(s, v, m_ref, l_ref, acc_ref, h):
    p, corr = _softmax_probs(s, m_ref, l_ref, h)
    _accumulate_pv(p, corr, v, acc_ref, h)


def _softmax_probs(s, m_ref, l_ref, h):
    slabs = _lane_slabs(s)
    m_prev = m_ref[h]
    m_new = jnp.maximum(m_prev, jnp.max(functools.reduce(jnp.maximum, slabs), -1, keepdims=True))
    corr = jnp.exp2(m_prev - m_new)
    p_slabs = [jnp.exp2(x - m_new) for x in slabs]
    l_ref[h] = l_ref[h] * corr + jnp.sum(functools.reduce(jnp.add, p_slabs), -1, keepdims=True)
    m_ref[h] = m_new
    return jnp.concatenate(p_slabs, 1).astype(BF16), corr


def _accumulate_pv(p, corr, v, acc_ref, h):
    pv = jnp.dot(p, v, preferred_element_type=F32)
    acc_ref[h] = jnp.concatenate(
        [a * corr + b for a, b in zip(_lane_slabs(acc_ref[h]), _lane_slabs(pv))], 1)


def _attn_prompt_kernel(q_ref, k_ref, v_ref, o_ref, m_ref, l_ref, acc_ref, *, tk):
    qi = pl.program_id(1)
    n_heads, tq, _ = q_ref.shape
    vd = v_ref.shape[2]
    m_ref[...] = jnp.full(m_ref.shape, -jnp.inf, F32)
    l_ref[...] = jnp.zeros(l_ref.shape, F32)
    acc_ref[...] = jnp.zeros(acc_ref.shape, F32)

    def chunk(start, diagonal):
        if diagonal:
            qpos = qi * tq + lax.broadcasted_iota(jnp.int32, (tq, tk), 0)
            visible = lax.broadcasted_iota(jnp.int32, (tq, tk), 1) + start <= qpos

        def scores(h):
            return _dot_nt(q_ref[h], k_ref[h, pl.ds(start, tk), :])

        s_q, p_q = {}, {}
        la = SCORE_LOOKAHEAD
        for h in range(n_heads + la + 1):
            if h < n_heads:
                s_q[h] = scores(h)
            if h >= la + 1:
                hp = h - la - 1
                _accumulate_pv(*p_q.pop(hp), v_ref[hp, pl.ds(start, tk), :], acc_ref, hp)
            if la <= h < n_heads + la:
                s = s_q.pop(h - la)
                if diagonal:
                    s = jnp.where(visible, s, -jnp.inf)
                p_q[h - la] = _softmax_probs(s, m_ref, l_ref, h - la)

    n_full = (qi * tq) // tk

    def body(c, carry):
        chunk(pl.multiple_of(c * tk, tk), False)
        return carry

    lax.fori_loop(0, n_full, body, 0)
    chunk(pl.multiple_of(n_full * tk, tk), True)
    for h in range(n_heads):
        o_ref[:, h * vd:(h + 1) * vd] = (acc_ref[h] * (1.0 / l_ref[h])).astype(o_ref.dtype)


def _attn_prompt_call(q, k, v, *, batch, seq, tq, tk):
    n_heads, t, qw = q.shape
    vd = v.shape[2]
    nq = seq // tq
    assert tk % tq == 0 and seq % tk == 0 and vd == LANES
    return pl.pallas_call(
        functools.partial(_attn_prompt_kernel, tk=tk),
        grid=(batch, nq),
        in_specs=[
            pl.BlockSpec((n_heads, tq, qw), lambda b, i: (0, b * nq + i, 0)),
            pl.BlockSpec((n_heads, seq, qw), lambda b, i: (0, b, 0), pipeline_mode=pl.Buffered(1)),
            pl.BlockSpec((n_heads, seq, vd), lambda b, i: (0, b, 0), pipeline_mode=pl.Buffered(1)),
        ],
        out_specs=pl.BlockSpec((tq, n_heads * vd), lambda b, i: (b * nq + i, 0)),
        out_shape=jax.ShapeDtypeStruct((t, n_heads * vd), BF16),
        scratch_shapes=[
            pltpu.VMEM((n_heads, tq, LANES), F32), pltpu.VMEM((n_heads, tq, LANES), F32),
            pltpu.VMEM((n_heads, tq, vd), F32),
        ],
        compiler_params=_cparams("parallel", "arbitrary"),
        name="attn_prompt",
    )(q, k, v)


def _attn_sample_kernel(pt_ref, q_ref, cnew_ref, krnew_ref, wuv_ref, cache_c, cache_krt, o_ref,
                        cbuf, krbuf, sem, m_ref, l_ref, acc_ref, kc_ref, krc_ref,
                        *, layer, pages, rank, rope):
    b, j = pl.program_id(0), pl.program_id(1)
    nj = pl.num_programs(1)
    step = b * nj + j
    last = pl.num_programs(0) * nj - 1
    slot = step % 2
    n_heads, tn, qw = q_ref.shape
    rows = n_heads * tn
    page = cbuf.shape[2]

    def page_copies(bb, jj, sl):
        out = []
        for g in range(pages):
            pg = pt_ref[bb, jj * pages + g]
            out.append(pltpu.make_async_copy(cache_c.at[layer, pg], cbuf.at[sl, g], sem.at[0, sl]))
            out.append(pltpu.make_async_copy(cache_krt.at[layer, pg], krbuf.at[sl, g], sem.at[1, sl]))
        return out

    def start_all(copies):
        for i, cp in enumerate(copies):
            cp.start(priority=(i // 2) % 2)

    @pl.when(step == 0)
    def _():
        start_all(page_copies(b, j, slot))

    @pl.when(step < last)
    def _():
        wrap = j + 1 == nj
        start_all(page_copies(jnp.where(wrap, b + 1, b), jnp.where(wrap, 0, j + 1), 1 - slot))

    q = q_ref[...].reshape(rows, qw)
    q_lat = q[:, :rank].astype(BF16)
    q_rope = q[:, rank:rank + rope].astype(BF16)
    nt = (((1,), (1,)), ((), ()))

    @pl.when(j == 0)
    def _():
        m_ref[...] = jnp.full(m_ref.shape, -jnp.inf, F32)
        l_ref[...] = jnp.zeros(l_ref.shape, F32)
        acc_ref[...] = jnp.zeros(acc_ref.shape, F32)
        pad = page - tn
        kc = jnp.concatenate([cnew_ref[...], jnp.zeros((pad, rank), F32)], 0).astype(BF16)
        krc = jnp.concatenate([krnew_ref[...], jnp.zeros((pad, rope), F32)], 0).astype(BF16)
        s = (lax.dot_general(q_lat, kc, nt, preferred_element_type=F32)
             + lax.dot_general(q_rope, krc, nt, preferred_element_type=F32))
        tok = lax.broadcasted_iota(jnp.int32, (rows, page), 0) % tn
        key = lax.broadcasted_iota(jnp.int32, (rows, page), 1)
        _softmax_update(jnp.where(key <= tok, s, -jnp.inf), kc, m_ref, l_ref, acc_ref, 0)

    for cp in page_copies(b, j, slot):
        cp.wait()

    gp = SAMPLE_PAGES_PER_SOFTMAX
    rows_c = gp * page

    def scores(c):
        for g in range(c * gp, (c + 1) * gp):
            kc_ref[g * page:(g + 1) * page, :] = cbuf[slot, g].astype(BF16)
            krc_ref[:, g * page:(g + 1) * page] = krbuf[slot, g].astype(BF16)
        return (lax.dot_general(q_lat, kc_ref[c * rows_c:(c + 1) * rows_c, :], nt,
                                preferred_element_type=F32)
                + jnp.dot(q_rope, krc_ref[:, c * rows_c:(c + 1) * rows_c],
                          preferred_element_type=F32))

    n_sub = pages // gp
    s_q, p_q = {}, {}
    for c in range(n_sub + 2):
        if c < n_sub:
            s_q[c] = scores(c)
        if c >= 2:
            _accumulate_pv(*p_q.pop(c - 2), kc_ref[(c - 2) * rows_c:(c - 1) * rows_c, :],
                           acc_ref, 0)
        if 1 <= c <= n_sub:
            p_q[c - 1] = _softmax_probs(s_q.pop(c - 1), m_ref, l_ref, 0)

    @pl.when(j == nj - 1)
    def _():
        inv = 1.0 / l_ref[0]
        o_lat = jnp.concatenate([a * inv for a in _lane_slabs(acc_ref[0])], 1).astype(BF16)
        vd = wuv_ref.shape[2]
        for h in range(n_heads):
            o_ref[:, h * vd:(h + 1) * vd] = jnp.dot(
                o_lat[h * tn:(h + 1) * tn], wuv_ref[h], preferred_element_type=F32
            ).astype(o_ref.dtype)


def _attn_sample_call(page_table, q, c_new, kr_new, wuv, cache_c, cache_krt, *, layer, tn):
    n_heads, t, qw = q.shape
    _, rank, vd = wuv.shape
    rope = kr_new.shape[1]
    bd, n_pages = page_table.shape
    page = cache_c.shape[2]
    pages = math.gcd(n_pages, MAX_PAGES_PER_STEP)
    assert pages % SAMPLE_PAGES_PER_SOFTMAX == 0
    rows = n_heads * tn
    grid_spec = pltpu.PrefetchScalarGridSpec(
        num_scalar_prefetch=1,
        grid=(bd, n_pages // pages),
        in_specs=[
            pl.BlockSpec((n_heads, tn, qw), lambda b, j, pt: (0, b, 0)),
            pl.BlockSpec((tn, rank), lambda b, j, pt: (b, 0)),
            pl.BlockSpec((tn, rope), lambda b, j, pt: (b, 0)),
            pl.BlockSpec((n_heads, rank, vd), lambda b, j, pt: (0, 0, 0)),
            pl.BlockSpec(memory_space=pl.ANY),
            pl.BlockSpec(memory_space=pl.ANY),
        ],
        out_specs=pl.BlockSpec((tn, n_heads * vd), lambda b, j, pt: (b, 0)),
        scratch_shapes=[
            pltpu.VMEM((2, pages, page, rank), F32),
            pltpu.VMEM((2, pages, rope, page), F32),
            pltpu.SemaphoreType.DMA((2, 2)),
            pltpu.VMEM((1, rows, LANES), F32), pltpu.VMEM((1, rows, LANES), F32),
            pltpu.VMEM((1, rows, rank), F32),
            pltpu.VMEM((pages * page, rank), BF16), pltpu.VMEM((rope, pages * page), BF16),
        ],
    )
    return pl.pallas_call(
        functools.partial(_attn_sample_kernel, layer=layer, pages=pages, rank=rank, rope=rope),
        grid_spec=grid_spec,
        out_shape=jax.ShapeDtypeStruct((t, n_heads * vd), BF16),
        compiler_params=_cparams("arbitrary", "arbitrary"),
        name="attn_sample",
    )(page_table, q, c_new, kr_new, wuv, cache_c, cache_krt)


def _merge_kernel(x_ref, a_ref, o_ref, wg_ref, wo_ref, g_ref, b_ref, x1_ref, *, alpha):
    x = x_ref[...]
    d = x.shape[1]
    gates = jax.nn.sigmoid(_dot_nt(x.astype(BF16), wg_ref[...]))
    h = gates[:, :d] * a_ref[...].astype(F32) + gates[:, d:] * o_ref[...].astype(F32)
    t = jnp.dot(h.astype(BF16), wo_ref[...], preferred_element_type=F32)
    x1_ref[...] = _layernorm(alpha * x + t, g_ref[...], b_ref[...])


def _merge_call(x, a, o, wg, wo, g, b, *, alpha, tm):
    t, d = x.shape
    tok = pl.BlockSpec((tm, d), lambda i: (i, 0))
    return pl.pallas_call(
        functools.partial(_merge_kernel, alpha=alpha),
        grid=(t // tm,),
        in_specs=[tok, tok, tok, _const_spec((2 * d, d)), _const_spec((d, d)),
                  _const_spec((1, d)), _const_spec((1, d))],
        out_specs=tok,
        out_shape=jax.ShapeDtypeStruct((t, d), F32),
        compiler_params=_cparams("parallel"),
        name="merge",
    )(x, a, o, wg, wo, g, b)


def _ffn_kernel(x1_ref, wup_ref, wdn_ref, g_ref, b_ref, y_ref, xb_ref, *, alpha):
    j = pl.program_id(1)

    @pl.when(j == 0)
    def _():
        xb_ref[...] = x1_ref[...].astype(BF16)
        y_ref[...] = jnp.zeros(y_ref.shape, F32)

    f = jnp.maximum(jnp.dot(xb_ref[...], wup_ref[...], preferred_element_type=F32), 0.0)
    y_ref[...] += jnp.dot((f * f).astype(BF16), wdn_ref[...], preferred_element_type=F32)

    @pl.when(j == pl.num_programs(1) - 1)
    def _():
        y_ref[...] = _layernorm(alpha * x1_ref[...] + y_ref[...], g_ref[...], b_ref[...])


def _ffn_call(x1, wup, wdn, g, b, *, alpha, tm, tf):
    t, d = x1.shape
    dff = wup.shape[1]
    return pl.pallas_call(
        functools.partial(_ffn_kernel, alpha=alpha),
        grid=(t // tm, dff // tf),
        in_specs=[
            pl.BlockSpec((tm, d), lambda i, j: (i, 0)),
            pl.BlockSpec((d, tf), lambda i, j: (0, j)),
            pl.BlockSpec((tf, d), lambda i, j: (j, 0)),
            _const_spec((1, d)), _const_spec((1, d)),
        ],
        out_specs=pl.BlockSpec((tm, d), lambda i, j: (i, 0)),
        out_shape=jax.ShapeDtypeStruct((t, d), F32),
        scratch_shapes=[pltpu.VMEM((tm, d), BF16)],
        compiler_params=_cparams("parallel", "arbitrary"),
        name="ffn",
    )(x1, wup, wdn, g, b)


def _rope_tables(pos, rope):
    inv = ROPE_THETA ** (-jnp.arange(0, rope, 2, dtype=F32) / rope)
    ang = pos.astype(F32)[:, None] * inv[None, :]
    cos, sin = jnp.cos(ang), jnp.sin(ang)
    reps = LANES // rope
    return (jnp.tile(jnp.concatenate([cos, cos], -1), (1, reps)),
            jnp.tile(jnp.concatenate([-sin, sin], -1), (1, reps)))


def _tile_rows(t, want):
    return want if t % want == 0 else t


def kernel(x_prompt, x_sample, cache_kv_latent, cache_k_rope, page_table, w_in, gmlp_ln_g, gmlp_ln_b, gmlp_ws, gmlp_bs, kv_norm_g, w_uk, w_uv, w_o, ln1_g, ln1_b, w_up, w_down, ln2_g, ln2_b):
    batch, seq, d = x_prompt.shape
    bd, tn, _ = x_sample.shape
    depth = w_in.shape[0]
    _, rank, n_heads, nope = w_uk.shape
    vd = w_uv.shape[3]
    rope = cache_k_rope.shape[-1]
    dg = gmlp_ln_g.shape[-1]
    groups, chunk = gmlp_ws.shape[1], gmlp_ws.shape[2]
    page = cache_kv_latent.shape[2]
    assert nope == LANES and 2 * rope == LANES and rank % LANES == 0 and dg == d
    assert n_heads % 2 == 0 and chunk % tn == 0 and page >= tn
    alpha = (2 * depth) ** 0.25
    scale = (nope + rope) ** -0.5 * math.log2(math.e)
    cache_krt = jnp.swapaxes(cache_k_rope, 2, 3)
    q_dim = n_heads * (nope + rope)
    o_q, o_c, o_kr, o_g = 2 * dg, 2 * dg + q_dim, 2 * dg + q_dim + rank, 2 * dg + q_dim + rank + rope


    past = page_table.shape[1] * page
    cos_p, sin_p = _rope_tables(jnp.arange(seq), rope)
    tm_s = _tile_rows(bd * tn, 256)
    cos_s, sin_s = _rope_tables(past + jnp.arange(tn), rope)
    cos_s, sin_s = jnp.tile(cos_s, (tm_s // tn, 1)), jnp.tile(sin_s, (tm_s // tn, 1))

    hp = x_prompt.reshape(batch * seq, d)
    hs = x_sample.reshape(bd * tn, d)
    tm_p = _tile_rows(batch * seq, 256)
    outs = {k: [] for k in ("c_p", "kr_p", "c_s", "kr_s", "v_s")}
    for l in range(depth):
        w_t = jnp.swapaxes(w_in[l], 0, 1).astype(BF16)
        wq_heads = w_t[o_q:o_c].reshape(n_heads, nope + rope, d)
        wq = jnp.concatenate([
            wq_heads[:, :nope].reshape(n_heads * nope, d),
            w_t[o_c:o_kr],
            wq_heads[:, nope:].reshape(n_heads * rope, d),
            w_t[o_kr:o_g],
            jnp.zeros((LANES - rope, d), BF16),
        ])
        wg = w_t[o_g:]
        wuk_h = jnp.transpose(w_uk[l], (1, 2, 0)).astype(BF16)
        wuv_h = jnp.transpose(w_uv[l], (1, 0, 2)).astype(BF16)
        wuk_all = w_uk[l].reshape(rank, n_heads * nope).astype(BF16)
        wuv_all = w_uv[l].reshape(rank, n_heads * vd).astype(BF16)
        wo, wup, wdn = w_o[l].astype(BF16), w_up[l].astype(BF16), w_down[l].astype(BF16)
        lng, lnb = gmlp_ln_g[l][None], gmlp_ln_b[l][None]
        kvg = kv_norm_g[l][None]
        gw = dg // groups
        ws_p = gmlp_ws[l]
        bs_p = jnp.broadcast_to(gmlp_bs[l][:, :, None], (groups, chunk, gw))
        rep = chunk // tn
        ws_s = jnp.tile(gmlp_ws[l][:, :tn, :tn], (1, rep, rep))
        bs_s = jnp.broadcast_to(jnp.tile(gmlp_bs[l][:, :tn], (1, rep))[:, :, None], (groups, chunk, gw))

        a = _gate_call(hp, w_t, lng, lnb, ws_p, bs_p, seq_len=min(seq, chunk), want_v=False, tm=tm_p)[0]
        q, c_kv, k_r, k_h, v_h = _qkv_call(hp, wq, kvg, cos_p, sin_p, wuk_all, wuv_all, q_dtype=BF16,
                                           table_blocks=seq // tm_p, scale=scale, tm=tm_p)
        o = _attn_prompt_call(q, k_h, v_h, batch=batch, seq=seq, tq=LANES, tk=256)
        x1 = _merge_call(hp, a, o, wg, wo, ln1_g[l][None], ln1_b[l][None], alpha=alpha, tm=tm_p)
        hp = _ffn_call(x1, wup, wdn, ln2_g[l][None], ln2_b[l][None], alpha=alpha,
                       tm=_tile_rows(batch * seq, 1024), tf=512)
        outs["c_p"].append(c_kv.reshape(batch, seq, rank))
        outs["kr_p"].append(k_r.reshape(batch, seq, rope))

        a, v = _gate_call(hs, w_t, lng, lnb, ws_s, bs_s, seq_len=tn, want_v=True, tm=tm_s)
        q, c_kv, k_r, _ = _qkv_call(hs, wq, kvg, cos_s, sin_s, wuk_h, q_dtype=F32,
                                    table_blocks=1, scale=scale, tm=tm_s)
        o = _attn_sample_call(page_table, q, c_kv, k_r, wuv_h, cache_kv_latent, cache_krt,
                              layer=l, tn=tn)
        x1 = _merge_call(hs, a, o, wg, wo, ln1_g[l][None], ln1_b[l][None], alpha=alpha, tm=tm_s)
        hs = _ffn_call(x1, wup, wdn, ln2_g[l][None], ln2_b[l][None], alpha=alpha,
                       tm=_tile_rows(bd * tn, 1024), tf=512)
        outs["c_s"].append(c_kv.reshape(bd, tn, rank))
        outs["kr_s"].append(k_r.reshape(bd, tn, rope))
        outs["v_s"].append(v.reshape(bd, tn, dg))

    return (hp.reshape(batch, seq, d), hs.reshape(bd, tn, d),
            jnp.stack(outs["c_p"]), jnp.stack(outs["kr_p"]),
            jnp.stack(outs["c_s"]), jnp.stack(outs["kr_s"]), jnp.stack(outs["v_s"]))
```

```python
import functools
import math

import jax
import jax.numpy as jnp
from jax import lax
from jax.experimental import pallas as pl
from jax.experimental.pallas import tpu as pltpu

F32 = jnp.float32
BF16 = jnp.bfloat16

LANES = 128
ROPE_THETA = 10000.0
LN_EPS = 1e-5
RMS_EPS = 1e-6
MAX_PAGES_PER_STEP = 64
SAMPLE_PAGES_PER_SOFTMAX = 4
SCORE_LOOKAHEAD = 3
VMEM_LIMIT = 56 * 1024 * 1024


def _cparams(*sem):
    return pltpu.CompilerParams(dimension_semantics=sem, vmem_limit_bytes=VMEM_LIMIT)


def _const_spec(shape):
    nd = len(shape)
    return pl.BlockSpec(shape, lambda *_: (0,) * nd, pipeline_mode=pl.Buffered(1))


def _gelu(x):
    return 0.5 * x * (1.0 + lax.erf(x * math.sqrt(0.5)))


def _layernorm(x, g, b):
    mu = jnp.mean(x, -1, keepdims=True)
    d = x - mu
    var = jnp.mean(d * d, -1, keepdims=True)
    return d * lax.rsqrt(var + LN_EPS) * g + b


def _dot_nt(x, w):
    return lax.dot_general(x, w, (((1,), (1,)), ((), ())), preferred_element_type=F32)


def _gate_kernel(x_ref, wuv_ref, lng_ref, lnb_ref, ws_ref, bs_ref, a_ref, *v_out,
                 seq_len, chunk, groups):
    x = x_ref[...].astype(BF16)
    d = x.shape[1]
    u = _gelu(_dot_nt(x, wuv_ref[:d, :]))
    v = _layernorm(_gelu(_dot_nt(x, wuv_ref[d:, :])), lng_ref[...], lnb_ref[...])
    if v_out:
        v_out[0][...] = v
    vb = v.astype(BF16)
    tm, d = v.shape
    gw = d // groups
    row = lax.broadcasted_iota(jnp.int32, (chunk, chunk), 0)
    col = lax.broadcasted_iota(jnp.int32, (chunk, chunk), 1)
    mask = (col <= row) & ((row // seq_len) == (col // seq_len))
    for g in range(groups):
        w = jnp.where(mask, ws_ref[g], 0.0).astype(BF16)
        for c in range(tm // chunk):
            rs, cs = slice(c * chunk, (c + 1) * chunk), slice(g * gw, (g + 1) * gw)
            mixed = jnp.dot(w, vb[rs, cs], preferred_element_type=F32) + bs_ref[g]
            a_ref[rs, cs] = (u[rs, cs] * mixed).astype(a_ref.dtype)


def _gate_call(x, w_t, lng, lnb, ws, bs, *, seq_len, want_v, tm):
    t, d = x.shape
    assert tm % ws.shape[1] == 0
    groups, chunk, _ = ws.shape
    gw = d // groups
    out_shape = [jax.ShapeDtypeStruct((t, d), BF16)]
    out_specs = [pl.BlockSpec((tm, d), lambda i: (i, 0))]
    if want_v:
        out_shape.append(jax.ShapeDtypeStruct((t, d), F32))
        out_specs.append(pl.BlockSpec((tm, d), lambda i: (i, 0)))
    return pl.pallas_call(
        functools.partial(_gate_kernel, seq_len=seq_len, chunk=chunk, groups=groups),
        grid=(t // tm,),
        in_specs=[
            pl.BlockSpec((tm, d), lambda i: (i, 0)),
            _const_spec((2 * d, d)),
            _const_spec((1, d)), _const_spec((1, d)),
            _const_spec((groups, chunk, chunk)), _const_spec((groups, chunk, gw)),
        ],
        out_specs=out_specs,
        out_shape=out_shape,
        compiler_params=_cparams("parallel"),
        name="gate",
    )(x, w_t, lng, lnb, ws, bs)


def _qkv_kernel(x_ref, w_ref, kvg_ref, cos_ref, sin_ref, wk_ref, *rest,
                n_heads, nope, rank, rope, scale, absorb):
    if absorb:
        q_ref, ckv_ref, kr_ref, kvb_ref = rest
    else:
        wv_ref, q_ref, ckv_ref, kr_ref, k_ref, v_ref = rest
    x = x_ref[...].astype(BF16)
    z = _dot_nt(x, w_ref[...])
    tm = z.shape[0]
    nope_w = n_heads * nope
    c = z[:, nope_w:nope_w + rank]
    c = c * lax.rsqrt(jnp.mean(c * c, -1, keepdims=True) + RMS_EPS) * kvg_ref[...]
    ckv_ref[...] = c
    cb = c.astype(BF16)

    off = nope_w + rank
    cos, sin = cos_ref[...], sin_ref[...]
    lane = lax.broadcasted_iota(jnp.int32, (tm, LANES), 1)
    first_half = (lane % rope) < (rope // 2)

    def rotate(xg):
        swapped = jnp.where(first_half, pltpu.roll(xg, LANES - rope // 2, 1),
                            pltpu.roll(xg, rope // 2, 1))
        return xg * cos + swapped * sin

    kr_off = off + LANES * (n_heads // 2)
    kr = rotate(z[:, kr_off:kr_off + LANES])
    kr_ref[...] = kr[:, :rope]
    krb = kr.astype(BF16)
    if absorb:
        kvb_ref[:, :rank] = cb
        kvb_ref[:, rank:] = krb
        qn_w = rank
    else:
        kn = jnp.dot(cb, wk_ref[...], preferred_element_type=F32)
        vn = jnp.dot(cb, wv_ref[...], preferred_element_type=F32)
        vd = vn.shape[1] // n_heads
        qn_w = nope

    for k in range(n_heads // 2):
        pair = rotate(z[:, off + LANES * k: off + LANES * (k + 1)]) * scale
        for h, slot in ((2 * k, pair), (2 * k + 1, pltpu.roll(pair, rope, 1))):
            qn = z[:, h * nope:(h + 1) * nope]
            if absorb:
                qn = jnp.dot(qn.astype(BF16), wk_ref[h], preferred_element_type=F32)
            else:
                k_ref[h, :, :nope] = kn[:, h * nope:(h + 1) * nope].astype(BF16)
                k_ref[h, :, nope:] = krb
                v_ref[h] = vn[:, h * vd:(h + 1) * vd].astype(BF16)
            q_ref[h, :, :qn_w] = (qn * scale).astype(q_ref.dtype)
            q_ref[h, :, qn_w:] = slot.astype(q_ref.dtype)


def _qkv_call(x, w, kvg, cos, sin, wk, wv=None, *, q_dtype, table_blocks, scale, tm):
    t, d = x.shape
    absorb = wv is None
    rope = LANES // 2
    ncols = w.shape[0]
    if absorb:
        n_heads, nope, rank = wk.shape
        qw = rank + LANES
        w_specs = [_const_spec(wk.shape)]
        extra_specs = [pl.BlockSpec((tm, qw), lambda i: (i, 0))]
        extra_shapes = [jax.ShapeDtypeStruct((t, qw), BF16)]
        weights = (wk,)
    else:
        rank = wk.shape[0]
        nope = LANES
        n_heads = wk.shape[1] // nope
        vd = wv.shape[1] // n_heads
        qw = nope + LANES
        w_specs = [_const_spec(wk.shape), _const_spec(wv.shape)]
        extra_specs = [pl.BlockSpec((n_heads, tm, qw), lambda i: (0, i, 0)),
                       pl.BlockSpec((n_heads, tm, vd), lambda i: (0, i, 0))]
        extra_shapes = [jax.ShapeDtypeStruct((n_heads, t, qw), BF16),
                        jax.ShapeDtypeStruct((n_heads, t, vd), BF16)]
        weights = (wk, wv)
    return pl.pallas_call(
        functools.partial(_qkv_kernel, n_heads=n_heads, nope=nope, rank=rank, rope=rope,
                          scale=scale, absorb=absorb),
        grid=(t // tm,),
        in_specs=[
            pl.BlockSpec((tm, d), lambda i: (i, 0)),
            _const_spec((ncols, d)), _const_spec((1, rank)),
            pl.BlockSpec((tm, LANES), lambda i: (i % table_blocks, 0)),
            pl.BlockSpec((tm, LANES), lambda i: (i % table_blocks, 0)),
        ] + w_specs,
        out_specs=[
            pl.BlockSpec((n_heads, tm, qw), lambda i: (0, i, 0)),
            pl.BlockSpec((tm, rank), lambda i: (i, 0)),
            pl.BlockSpec((tm, rope), lambda i: (i, 0)),
        ] + extra_specs,
        out_shape=[
            jax.ShapeDtypeStruct((n_heads, t, qw), q_dtype),
            jax.ShapeDtypeStruct((t, rank), F32),
            jax.ShapeDtypeStruct((t, rope), F32),
        ] + extra_shapes,
        compiler_params=_cparams("parallel"),
        name="qkv",
    )(x, w, kvg, cos, sin, *weights)


def _lane_slabs(x):
    return [x[:, c:c + LANES] for c in range(0, x.shape[1], LANES)]


def _softmax_update(s, v, m_ref, l_ref, acc_ref, h):
    p, corr = _softmax_probs(s, m_ref, l_ref, h)
    _accumulate_pv(p, corr, v, acc_ref, h)


def _softmax_probs(s, m_ref, l_ref, h):
    slabs = _lane_slabs(s)
    m_prev = m_ref[h]
    m_new = jnp.maximum(m_prev, jnp.max(functools.reduce(jnp.maximum, slabs), -1, keepdims=True))
    corr = jnp.exp2(m_prev - m_new)
    p_slabs = [jnp.exp2(x - m_new) for x in slabs]
    l_ref[h] = l_ref[h] * corr + jnp.sum(functools.reduce(jnp.add, p_slabs), -1, keepdims=True)
    m_ref[h] = m_new
    return jnp.concatenate(p_slabs, 1).astype(BF16), corr


def _accumulate_pv(p, corr, v, acc_ref, h):
    pv = jnp.dot(p, v, preferred_element_type=F32)
    acc_ref[h] = jnp.concatenate(
        [a * corr + b for a, b in zip(_lane_slabs(acc_ref[h]), _lane_slabs(pv))], 1)


def _attn_prompt_kernel(q_ref, k_ref, v_ref, o_ref, m_ref, l_ref, acc_ref, *, tk):
    qi = pl.program_id(1)
    n_heads, tq, _ = q_ref.shape
    vd = v_ref.shape[2]
    m_ref[...] = jnp.full(m_ref.shape, -jnp.inf, F32)
    l_ref[...] = jnp.zeros(l_ref.shape, F32)
    acc_ref[...] = jnp.zeros(acc_ref.shape, F32)

    def chunk(start, diagonal):
        if diagonal:
            qpos = qi * tq + lax.broadcasted_iota(jnp.int32, (tq, tk), 0)
            visible = lax.broadcasted_iota(jnp.int32, (tq, tk), 1) + start <= qpos

        def scores(h):
            return _dot_nt(q_ref[h], k_ref[h, pl.ds(start, tk), :])

        s_q, p_q = {}, {}
        la = SCORE_LOOKAHEAD
        for h in range(n_heads + la + 1):
            if h < n_heads:
                s_q[h] = scores(h)
            if h >= la + 1:
                hp = h - la - 1
                _accumulate_pv(*p_q.pop(hp), v_ref[hp, pl.ds(start, tk), :], acc_ref, hp)
            if la <= h < n_heads + la:
                s = s_q.pop(h - la)
                if diagonal:
                    s = jnp.where(visible, s, -jnp.inf)
                p_q[h - la] = _softmax_probs(s, m_ref, l_ref, h - la)

    n_full = (qi * tq) // tk

    def body(c, carry):
        chunk(pl.multiple_of(c * tk, tk), False)
        return carry

    lax.fori_loop(0, n_full, body, 0)
    chunk(pl.multiple_of(n_full * tk, tk), True)
    for h in range(n_heads):
        o_ref[:, h * vd:(h + 1) * vd] = (acc_ref[h] * (1.0 / l_ref[h])).astype(o_ref.dtype)


def _attn_prompt_call(q, k, v, *, batch, seq, tq, tk):
    n_heads, t, qw = q.shape
    vd = v.shape[2]
    nq = seq // tq
    assert tk % tq == 0 and seq % tk == 0 and vd == LANES
    return pl.pallas_call(
        functools.partial(_attn_prompt_kernel, tk=tk),
        grid=(batch, nq),
        in_specs=[
            pl.BlockSpec((n_heads, tq, qw), lambda b, i: (0, b * nq + i, 0)),
            pl.BlockSpec((n_heads, seq, qw), lambda b, i: (0, b, 0), pipeline_mode=pl.Buffered(1)),
            pl.BlockSpec((n_heads, seq, vd), lambda b, i: (0, b, 0), pipeline_mode=pl.Buffered(1)),
        ],
        out_specs=pl.BlockSpec((tq, n_heads * vd), lambda b, i: (b * nq + i, 0)),
        out_shape=jax.ShapeDtypeStruct((t, n_heads * vd), BF16),
        scratch_shapes=[
            pltpu.VMEM((n_heads, tq, LANES), F32), pltpu.VMEM((n_heads, tq, LANES), F32),
            pltpu.VMEM((n_heads, tq, vd), F32),
        ],
        compiler_params=_cparams("parallel", "arbitrary"),
        name="attn_prompt",
    )(q, k, v)


def _attn_sample_kernel(pt_ref, q_ref, cnew_ref, krnew_ref, wuv_ref, cache_c, cache_krt, o_ref,
                        cbuf, krbuf, sem, m_ref, l_ref, acc_ref, kc_ref, krc_ref,
                        *, layer, pages, rank, rope):
    b, j = pl.program_id(0), pl.program_id(1)
    nj = pl.num_programs(1)
    step = b * nj + j
    last = pl.num_programs(0) * nj - 1
    slot = step % 2
    n_heads, tn, qw = q_ref.shape
    rows = n_heads * tn
    page = cbuf.shape[2]

    gp = SAMPLE_PAGES_PER_SOFTMAX
    n_sub = pages // gp

    def group_copies(bb, jj, sl, c):
        out = []
        for g in range(c * gp, (c + 1) * gp):
            pg = pt_ref[bb, jj * pages + g]
            out.append(pltpu.make_async_copy(cache_c.at[layer, pg], cbuf.at[sl, g], sem.at[0, sl, c]))
            out.append(pltpu.make_async_copy(cache_krt.at[layer, pg], krbuf.at[sl, g], sem.at[1, sl, c]))
        return out

    def start_group(bb, jj, sl, c):
        for i, cp in enumerate(group_copies(bb, jj, sl, c)):
            cp.start(priority=(i // 2) % 2)

    @pl.when(step == 0)
    def _():
        for c in range(n_sub):
            start_group(b, j, slot, c)

    is_last = step == last
    wrap = j + 1 == nj
    b_next = jnp.where(is_last, b, jnp.where(wrap, b + 1, b))
    j_next = jnp.where(is_last, j, jnp.where(wrap, 0, j + 1))

    q = q_ref[...].reshape(rows, qw)
    q_lat = q[:, :rank].astype(BF16)
    q_rope = q[:, rank:rank + rope].astype(BF16)
    nt = (((1,), (1,)), ((), ()))

    @pl.when(j == 0)
    def _():
        m_ref[...] = jnp.full(m_ref.shape, -jnp.inf, F32)
        l_ref[...] = jnp.zeros(l_ref.shape, F32)
        acc_ref[...] = jnp.zeros(acc_ref.shape, F32)
        pad = page - tn
        kc = jnp.concatenate([cnew_ref[...], jnp.zeros((pad, rank), F32)], 0).astype(BF16)
        krc = jnp.concatenate([krnew_ref[...], jnp.zeros((pad, rope), F32)], 0).astype(BF16)
        s = (lax.dot_general(q_lat, kc, nt, preferred_element_type=F32)
             + lax.dot_general(q_rope, krc, nt, preferred_element_type=F32))
        tok = lax.broadcasted_iota(jnp.int32, (rows, page), 0) % tn
        key = lax.broadcasted_iota(jnp.int32, (rows, page), 1)
        _softmax_update(jnp.where(key <= tok, s, -jnp.inf), kc, m_ref, l_ref, acc_ref, 0)

    rows_c = gp * page

    def scores(c):
        for cp in group_copies(b, j, slot, c):
            cp.wait()
        for g in range(c * gp, (c + 1) * gp):
            kc_ref[g * page:(g + 1) * page, :] = cbuf[slot, g].astype(BF16)
            krc_ref[:, g * page:(g + 1) * page] = krbuf[slot, g].astype(BF16)
        return (lax.dot_general(q_lat, kc_ref[c * rows_c:(c + 1) * rows_c, :], nt,
                                preferred_element_type=F32)
                + jnp.dot(q_rope, krc_ref[:, c * rows_c:(c + 1) * rows_c],
                          preferred_element_type=F32))

    s_q, p_q = {}, {}
    for c in range(n_sub + 2):
        if c < n_sub:
            s_q[c] = scores(c)
            start_group(b_next, j_next, 1 - slot, c)
        if c >= 2:
            _accumulate_pv(*p_q.pop(c - 2), kc_ref[(c - 2) * rows_c:(c - 1) * rows_c, :],
                           acc_ref, 0)
        if 1 <= c <= n_sub:
            p_q[c - 1] = _softmax_probs(s_q.pop(c - 1), m_ref, l_ref, 0)

    @pl.when(is_last)
    def _():
        for c in range(n_sub):
            for cp in group_copies(b_next, j_next, 1 - slot, c):
                cp.wait()

    @pl.when(j == nj - 1)
    def _():
        inv = 1.0 / l_ref[0]
        o_lat = jnp.concatenate([a * inv for a in _lane_slabs(acc_ref[0])], 1).astype(BF16)
        vd = wuv_ref.shape[2]
        for h in range(n_heads):
            o_ref[:, h * vd:(h + 1) * vd] = jnp.dot(
                o_lat[h * tn:(h + 1) * tn], wuv_ref[h], preferred_element_type=F32
            ).astype(o_ref.dtype)


def _attn_sample_call(page_table, q, c_new, kr_new, wuv, cache_c, cache_krt, *, layer, tn):
    n_heads, t, qw = q.shape
    _, rank, vd = wuv.shape
    rope = kr_new.shape[1]
    bd, n_pages = page_table.shape
    page = cache_c.shape[2]
    pages = math.gcd(n_pages, MAX_PAGES_PER_STEP)
    assert pages % SAMPLE_PAGES_PER_SOFTMAX == 0
    rows = n_heads * tn
    grid_spec = pltpu.PrefetchScalarGridSpec(
        num_scalar_prefetch=1,
        grid=(bd, n_pages // pages),
        in_specs=[
            pl.BlockSpec((n_heads, tn, qw), lambda b, j, pt: (0, b, 0)),
            pl.BlockSpec((tn, rank), lambda b, j, pt: (b, 0)),
            pl.BlockSpec((tn, rope), lambda b, j, pt: (b, 0)),
            pl.BlockSpec((n_heads, rank, vd), lambda b, j, pt: (0, 0, 0)),
            pl.BlockSpec(memory_space=pl.ANY),
            pl.BlockSpec(memory_space=pl.ANY),
        ],
        out_specs=pl.BlockSpec((tn, n_heads * vd), lambda b, j, pt: (b, 0)),
        scratch_shapes=[
            pltpu.VMEM((2, pages, page, rank), F32),
            pltpu.VMEM((2, pages, rope, page), F32),
            pltpu.SemaphoreType.DMA((2, 2, pages // SAMPLE_PAGES_PER_SOFTMAX)),
            pltpu.VMEM((1, rows, LANES), F32), pltpu.VMEM((1, rows, LANES), F32),
            pltpu.VMEM((1, rows, rank), F32),
            pltpu.VMEM((pages * page, rank), BF16), pltpu.VMEM((rope, pages * page), BF16),
        ],
    )
    return pl.pallas_call(
        functools.partial(_attn_sample_kernel, layer=layer, pages=pages, rank=rank, rope=rope),
        grid_spec=grid_spec,
        out_shape=jax.ShapeDtypeStruct((t, n_heads * vd), BF16),
        compiler_params=_cparams("arbitrary", "arbitrary"),
        name="attn_sample",
    )(page_table, q, c_new, kr_new, wuv, cache_c, cache_krt)


def _merge_kernel(x_ref, a_ref, o_ref, wg_ref, wo_ref, g_ref, b_ref, x1_ref, *, alpha):
    x = x_ref[...]
    d = x.shape[1]
    gates = jax.nn.sigmoid(_dot_nt(x.astype(BF16), wg_ref[...]))
    h = gates[:, :d] * a_ref[...].astype(F32) + gates[:, d:] * o_ref[...].astype(F32)
    t = jnp.dot(h.astype(BF16), wo_ref[...], preferred_element_type=F32)
    x1_ref[...] = _layernorm(alpha * x + t, g_ref[...], b_ref[...])


def _merge_call(x, a, o, wg, wo, g, b, *, alpha, tm):
    t, d = x.shape
    tok = pl.BlockSpec((tm, d), lambda i: (i, 0))
    return pl.pallas_call(
        functools.partial(_merge_kernel, alpha=alpha),
        grid=(t // tm,),
        in_specs=[tok, tok, tok, _const_spec((2 * d, d)), _const_spec((d, d)),
                  _const_spec((1, d)), _const_spec((1, d))],
        out_specs=tok,
        out_shape=jax.ShapeDtypeStruct((t, d), F32),
        compiler_params=_cparams("parallel"),
        name="merge",
    )(x, a, o, wg, wo, g, b)


def _ffn_kernel(x1_ref, wup_ref, wdn_ref, g_ref, b_ref, y_ref, xb_ref, *, alpha):
    j = pl.program_id(1)

    @pl.when(j == 0)
    def _():
        xb_ref[...] = x1_ref[...].astype(BF16)
        y_ref[...] = jnp.zeros(y_ref.shape, F32)

    f = jnp.maximum(jnp.dot(xb_ref[...], wup_ref[...], preferred_element_type=F32), 0.0)
    y_ref[...] += jnp.dot((f * f).astype(BF16), wdn_ref[...], preferred_element_type=F32)

    @pl.when(j == pl.num_programs(1) - 1)
    def _():
        y_ref[...] = _layernorm(alpha * x1_ref[...] + y_ref[...], g_ref[...], b_ref[...])


def _ffn_call(x1, wup, wdn, g, b, *, alpha, tm, tf):
    t, d = x1.shape
    dff = wup.shape[1]
    return pl.pallas_call(
        functools.partial(_ffn_kernel, alpha=alpha),
        grid=(t // tm, dff // tf),
        in_specs=[
            pl.BlockSpec((tm, d), lambda i, j: (i, 0)),
            pl.BlockSpec((d, tf), lambda i, j: (0, j)),
            pl.BlockSpec((tf, d), lambda i, j: (j, 0)),
            _const_spec((1, d)), _const_spec((1, d)),
        ],
        out_specs=pl.BlockSpec((tm, d), lambda i, j: (i, 0)),
        out_shape=jax.ShapeDtypeStruct((t, d), F32),
        scratch_shapes=[pltpu.VMEM((tm, d), BF16)],
        compiler_params=_cparams("parallel", "arbitrary"),
        name="ffn",
    )(x1, wup, wdn, g, b)


def _rope_tables(pos, rope):
    inv = ROPE_THETA ** (-jnp.arange(0, rope, 2, dtype=F32) / rope)
    ang = pos.astype(F32)[:, None] * inv[None, :]
    cos, sin = jnp.cos(ang), jnp.sin(ang)
    reps = LANES // rope
    return (jnp.tile(jnp.concatenate([cos, cos], -1), (1, reps)),
            jnp.tile(jnp.concatenate([-sin, sin], -1), (1, reps)))


def _tile_rows(t, want):
    return want if t % want == 0 else t


def kernel(x_prompt, x_sample, cache_kv_latent, cache_k_rope, page_table, w_in, gmlp_ln_g, gmlp_ln_b, gmlp_ws, gmlp_bs, kv_norm_g, w_uk, w_uv, w_o, ln1_g, ln1_b, w_up, w_down, ln2_g, ln2_b):
    batch, seq, d = x_prompt.shape
    bd, tn, _ = x_sample.shape
    depth = w_in.shape[0]
    _, rank, n_heads, nope = w_uk.shape
    vd = w_uv.shape[3]
    rope = cache_k_rope.shape[-1]
    dg = gmlp_ln_g.shape[-1]
    groups, chunk = gmlp_ws.shape[1], gmlp_ws.shape[2]
    page = cache_kv_latent.shape[2]
    assert nope == LANES and 2 * rope == LANES and rank % LANES == 0 and dg == d
    assert n_heads % 2 == 0 and chunk % tn == 0 and page >= tn
    alpha = (2 * depth) ** 0.25
    scale = (nope + rope) ** -0.5 * math.log2(math.e)
    cache_krt = jnp.swapaxes(cache_k_rope, 2, 3)
    q_dim = n_heads * (nope + rope)
    o_q, o_c, o_kr, o_g = 2 * dg, 2 * dg + q_dim, 2 * dg + q_dim + rank, 2 * dg + q_dim + rank + rope


    past = page_table.shape[1] * page
    cos_p, sin_p = _rope_tables(jnp.arange(seq), rope)
    tm_s = _tile_rows(bd * tn, 256)
    cos_s, sin_s = _rope_tables(past + jnp.arange(tn), rope)
    cos_s, sin_s = jnp.tile(cos_s, (tm_s // tn, 1)), jnp.tile(sin_s, (tm_s // tn, 1))

    hp = x_prompt.reshape(batch * seq, d)
    hs = x_sample.reshape(bd * tn, d)
    tm_p = _tile_rows(batch * seq, 256)
    outs = {k: [] for k in ("c_p", "kr_p", "c_s", "kr_s", "v_s")}
    for l in range(depth):
        w_t = jnp.swapaxes(w_in[l], 0, 1).astype(BF16)
        wq_heads = w_t[o_q:o_c].reshape(n_heads, nope + rope, d)
        wq = jnp.concatenate([
            wq_heads[:, :nope].reshape(n_heads * nope, d),
            w_t[o_c:o_kr],
            wq_heads[:, nope:].reshape(n_heads * rope, d),
            w_t[o_kr:o_g],
            jnp.zeros((LANES - rope, d), BF16),
        ])
        wg = w_t[o_g:]
        wuk_h = jnp.transpose(w_uk[l], (1, 2, 0)).astype(BF16)
        wuv_h = jnp.transpose(w_uv[l], (1, 0, 2)).astype(BF16)
        wuk_all = w_uk[l].reshape(rank, n_heads * nope).astype(BF16)
        wuv_all = w_uv[l].reshape(rank, n_heads * vd).astype(BF16)
        wo, wup, wdn = w_o[l].astype(BF16), w_up[l].astype(BF16), w_down[l].astype(BF16)
        lng, lnb = gmlp_ln_g[l][None], gmlp_ln_b[l][None]
        kvg = kv_norm_g[l][None]
        gw = dg // groups
        ws_p = gmlp_ws[l]
        bs_p = jnp.broadcast_to(gmlp_bs[l][:, :, None], (groups, chunk, gw))
        rep = chunk // tn
        ws_s = jnp.tile(gmlp_ws[l][:, :tn, :tn], (1, rep, rep))
        bs_s = jnp.broadcast_to(jnp.tile(gmlp_bs[l][:, :tn], (1, rep))[:, :, None], (groups, chunk, gw))

        a = _gate_call(hp, w_t, lng, lnb, ws_p, bs_p, seq_len=min(seq, chunk), want_v=False, tm=tm_p)[0]
        q, c_kv, k_r, k_h, v_h = _qkv_call(hp, wq, kvg, cos_p, sin_p, wuk_all, wuv_all, q_dtype=BF16,
                                           table_blocks=seq // tm_p, scale=scale, tm=tm_p)
        o = _attn_prompt_call(q, k_h, v_h, batch=batch, seq=seq, tq=LANES, tk=256)
        x1 = _merge_call(hp, a, o, wg, wo, ln1_g[l][None], ln1_b[l][None], alpha=alpha, tm=tm_p)
        hp = _ffn_call(x1, wup, wdn, ln2_g[l][None], ln2_b[l][None], alpha=alpha,
                       tm=_tile_rows(batch * seq, 1024), tf=512)
        outs["c_p"].append(c_kv.reshape(batch, seq, rank))
        outs["kr_p"].append(k_r.reshape(batch, seq, rope))

        a, v = _gate_call(hs, w_t, lng, lnb, ws_s, bs_s, seq_len=tn, want_v=True, tm=tm_s)
        q, c_kv, k_r, _ = _qkv_call(hs, wq, kvg, cos_s, sin_s, wuk_h, q_dtype=F32,
                                    table_blocks=1, scale=scale, tm=tm_s)
        o = _attn_sample_call(page_table, q, c_kv, k_r, wuv_h, cache_kv_latent, cache_krt,
                              layer=l, tn=tn)
        x1 = _merge_call(hs, a, o, wg, wo, ln1_g[l][None], ln1_b[l][None], alpha=alpha, tm=tm_s)
        hs = _ffn_call(x1, wup, wdn, ln2_g[l][None], ln2_b[l][None], alpha=alpha,
                       tm=_tile_rows(bd * tn, 1024), tf=512)
        outs["c_s"].append(c_kv.reshape(bd, tn, rank))
        outs["kr_s"].append(k_r.reshape(bd, tn, rope))
        outs["v_s"].append(v.reshape(bd, tn, dg))

    return (hp.reshape(batch, seq, d), hs.reshape(bd, tn, d),
            jnp.stack(outs["c_p"]), jnp.stack(outs["kr_p"]),
            jnp.stack(outs["c_s"]), jnp.stack(outs["kr_s"]), jnp.stack(outs["v_s"]))
```
